```python
import jax, jax.numpy as jnp
from jax import lax
import numpy as np

D_MODEL = 1024
BATCH = 32
SEQ = 2048
DEPTH = 4

CHUNK = 64
N_MIXERS = 2
D_FF = 256 * ((8 * D_MODEL // 3 + 255) // 256)
RMS_EPS = 1e-6
L2_EPS = 1e-6
SC_WIDTH = 3
GDN_HEAD_DIM = 128
GDN_K_HEADS = D_MODEL // 128
GDN_V_HEADS = 2 * GDN_K_HEADS
GDN_KEY_DIM = GDN_K_HEADS * GDN_HEAD_DIM
GDN_VAL_DIM = GDN_V_HEADS * GDN_HEAD_DIM
GDN_CONV_WIDTH = 4
GDN_CONV_DIM = 2 * GDN_KEY_DIM + GDN_VAL_DIM
GDN_IN_DIM = GDN_CONV_DIM + GDN_VAL_DIM + 2 * GDN_V_HEADS
N_SC_LAYERS = (DEPTH + 1) // 2
N_GDN_LAYERS = DEPTH // 2
N_NORMS = 6

kernel_name = "hybrid_shortconv_gdn_macaron_trunk"


def rms_norm(x, g):
    xf = x.astype(jnp.float32)
    y = xf * lax.rsqrt(jnp.mean(xf * xf, axis=-1, keepdims=True) + RMS_EPS)
    return (y * g.astype(jnp.float32)).astype(x.dtype)


def causal_depthwise_conv(x, w):
    k, c = w.shape
    return lax.conv_general_dilated(
        x, w[:, None, :].astype(x.dtype), window_strides=(1,), padding=[(k - 1, 0)],
        dimension_numbers=("NWC", "WIO", "NWC"), feature_group_count=c)


def swiglu(x, w_gate_up, w_down):
    gate, up = jnp.split(x @ w_gate_up, 2, axis=-1)
    return (jax.nn.silu(gate) * up) @ w_down


def short_conv_mixer(x, w_in, conv_w, w_out):
    b, c, v = jnp.split(x @ w_in, 3, axis=-1)
    y = b * causal_depthwise_conv(c * v, conv_w)
    return y @ w_out


def l2norm(x):
    return x * lax.rsqrt(jnp.sum(x * x, axis=-1, keepdims=True) + L2_EPS)


def chunk_gated_delta_rule(q, k, v, g, beta):
    bsz, nh, t, dk = q.shape
    dv = v.shape[-1]
    n = t // CHUNK
    q = q * (dk ** -0.5)

    def rs(a):
        return a.reshape(bsz, nh, n, CHUNK, *a.shape[3:])

    q, k, v, g, beta = rs(q), rs(k), rs(v), rs(g), rs(beta)
    g = jnp.cumsum(g, axis=-1)
    k_beta = k * beta[..., None]
    v_beta = v * beta[..., None]
    idx = jnp.arange(CHUNK)
    causal = idx[:, None] >= idx[None, :]
    strict = idx[:, None] > idx[None, :]
    decay = jnp.exp(jnp.where(causal, g[..., :, None] - g[..., None, :], -jnp.inf))
    a = jnp.where(strict, jnp.einsum("bhncd,bhnsd->bhncs", k_beta, k) * decay, 0.0)
    eye = jnp.eye(CHUNK, dtype=q.dtype)
    t_inv = lax.linalg.triangular_solve(eye + a, jnp.broadcast_to(eye, a.shape),
                                        left_side=True, lower=True)
    u = jnp.einsum("bhncs,bhnse->bhnce", t_inv, v_beta)
    w = jnp.einsum("bhncs,bhnsd->bhncd", t_inv, k_beta * jnp.exp(g)[..., None])
    qk = jnp.where(causal, jnp.einsum("bhncd,bhnsd->bhncs", q, k) * decay, 0.0)
    q_g = q * jnp.exp(g)[..., None]
    g_last = g[..., -1]
    k_tail = k * jnp.exp(g_last[..., None] - g)[..., None]

    def step(state, inp):
        q_g_n, qk_n, u_n, w_n, k_tail_n, gl_n = inp
        v_new = u_n - jnp.einsum("bhcd,bhde->bhce", w_n, state)
        o = jnp.einsum("bhcd,bhde->bhce", q_g_n, state) + jnp.einsum("bhcs,bhse->bhce", qk_n, v_new)
        state = state * jnp.exp(gl_n)[..., None, None] + jnp.einsum("bhcd,bhce->bhde", k_tail_n, v_new)
        return state, o

    xs = tuple(jnp.moveaxis(a_, 2, 0) for a_ in (q_g, qk, u, w, k_tail, g_last))
    s0 = jnp.zeros((bsz, nh, dk, dv), dtype=q.dtype)
    _, o = lax.scan(step, s0, xs)
    return jnp.moveaxis(o, 0, 2).reshape(bsz, nh, t, dv)


def gated_deltanet_mixer(x, w_in, conv_w, a_log, dt_bias, norm_w, w_out):
    bsz, t, _ = x.shape
    proj = x @ w_in
    qkv, z, b, a = jnp.split(proj, [GDN_CONV_DIM, GDN_CONV_DIM + GDN_VAL_DIM,
                                    GDN_CONV_DIM + GDN_VAL_DIM + GDN_V_HEADS], axis=-1)
    qkv = jax.nn.silu(causal_depthwise_conv(qkv, conv_w))
    q, k, v = jnp.split(qkv.astype(jnp.float32), [GDN_KEY_DIM, 2 * GDN_KEY_DIM], axis=-1)
    rep = GDN_V_HEADS // GDN_K_HEADS
    q = jnp.repeat(l2norm(q.reshape(bsz, t, GDN_K_HEADS, GDN_HEAD_DIM)), rep, axis=2)
    k = jnp.repeat(l2norm(k.reshape(bsz, t, GDN_K_HEADS, GDN_HEAD_DIM)), rep, axis=2)
    v = v.reshape(bsz, t, GDN_V_HEADS, GDN_HEAD_DIM)
    beta = jax.nn.sigmoid(b.astype(jnp.float32))
    g = -jnp.exp(a_log.astype(jnp.float32)) * jax.nn.softplus(
        a.astype(jnp.float32) + dt_bias.astype(jnp.float32))
    o = chunk_gated_delta_rule(q.transpose(0, 2, 1, 3), k.transpose(0, 2, 1, 3),
                               v.transpose(0, 2, 1, 3), g.transpose(0, 2, 1),
                               beta.transpose(0, 2, 1))
    o = o.transpose(0, 2, 1, 3)
    o = o * lax.rsqrt(jnp.mean(o * o, axis=-1, keepdims=True) + RMS_EPS) * norm_w.astype(jnp.float32)
    o = o * jax.nn.silu(z.astype(jnp.float32).reshape(bsz, t, GDN_V_HEADS, GDN_HEAD_DIM))
    return o.reshape(bsz, t, GDN_VAL_DIM).astype(x.dtype) @ w_out


def setup_inputs(seed: int = 0) -> dict:
    key = jax.random.key(seed)
    ks = jax.random.split(key, 16)
    f32 = jnp.float32
    d = D_MODEL
    x = jax.random.normal(ks[0], (BATCH, SEQ, d), f32)
    norm_g = 1.0 + 0.02 * jax.random.normal(ks[1], (DEPTH, N_NORMS, d), f32)
    ffn_w_gate_up = jax.random.normal(ks[2], (DEPTH, 2, d, 2 * D_FF), f32) * d ** -0.5
    ffn_w_down = jax.random.normal(ks[3], (DEPTH, 2, D_FF, d), f32) * D_FF ** -0.5
    sc_w_in = jax.random.normal(ks[4], (N_SC_LAYERS, d, 3 * d), f32) * d ** -0.5
    sc_conv_w = jax.random.normal(ks[5], (N_SC_LAYERS, SC_WIDTH, d), f32) * SC_WIDTH ** -0.5
    sc_w_out = jax.random.normal(ks[6], (N_SC_LAYERS, d, d), f32) * d ** -0.5
    gdn_w_in = jax.random.normal(ks[7], (N_GDN_LAYERS, d, GDN_IN_DIM), f32) * d ** -0.5
    gdn_conv_w = jax.random.normal(ks[8], (N_GDN_LAYERS, GDN_CONV_WIDTH, GDN_CONV_DIM), f32) * GDN_CONV_WIDTH ** -0.5
    gdn_a_log = jnp.log(jax.random.uniform(ks[9], (N_GDN_LAYERS, GDN_V_HEADS), f32, 1.0, 16.0))
    gdn_dt_bias = 1.0 + 0.1 * jax.random.normal(ks[10], (N_GDN_LAYERS, GDN_V_HEADS), f32)
    gdn_norm_w = 1.0 + 0.02 * jax.random.normal(ks[11], (N_GDN_LAYERS, GDN_HEAD_DIM), f32)
    gdn_w_out = jax.random.normal(ks[12], (N_GDN_LAYERS, GDN_VAL_DIM, d), f32) * GDN_VAL_DIM ** -0.5
    return {"x": x, "norm_g": norm_g, "ffn_w_gate_up": ffn_w_gate_up, "ffn_w_down": ffn_w_down,
            "sc_w_in": sc_w_in, "sc_conv_w": sc_conv_w, "sc_w_out": sc_w_out,
            "gdn_w_in": gdn_w_in, "gdn_conv_w": gdn_conv_w, "gdn_a_log": gdn_a_log,
            "gdn_dt_bias": gdn_dt_bias, "gdn_norm_w": gdn_norm_w, "gdn_w_out": gdn_w_out}


def reference(x, norm_g, ffn_w_gate_up, ffn_w_down, sc_w_in, sc_conv_w, sc_w_out,
              gdn_w_in, gdn_conv_w, gdn_a_log, gdn_dt_bias, gdn_norm_w, gdn_w_out):
    h = x
    for i in range(DEPTH):
        g = norm_g[i]
        ff = swiglu(rms_norm(h, g[0]), ffn_w_gate_up[i, 0], ffn_w_down[i, 0])
        h = h + 0.5 * rms_norm(ff, g[1])
        m_in = rms_norm(h, g[2])
        j = i // N_MIXERS
        if i % N_MIXERS == 0:
            mix = short_conv_mixer(m_in, sc_w_in[j], sc_conv_w[j], sc_w_out[j])
        else:
            mix = gated_deltanet_mixer(m_in, gdn_w_in[j], gdn_conv_w[j], gdn_a_log[j],
                                       gdn_dt_bias[j], gdn_norm_w[j], gdn_w_out[j])
        h = h + rms_norm(mix, g[3])
        ff = swiglu(rms_norm(h, g[4]), ffn_w_gate_up[i, 1], ffn_w_down[i, 1])
        h = h + 0.5 * rms_norm(ff, g[5])
    return h
```

```python
import functools

import jax
import jax.numpy as jnp
from jax import lax
from jax.experimental import pallas as pl
from jax.experimental.pallas import tpu as pltpu

F32 = jnp.float32
BF16 = jnp.bfloat16

RMS_EPS = 1e-6
L2_EPS = 1e-6
LANES = 128
HEAD_DIM = 128
K_HEADS = 8
V_HEADS = 16
KEY_DIM = K_HEADS * HEAD_DIM
VAL_DIM = V_HEADS * HEAD_DIM
CONV_DIM = 2 * KEY_DIM + VAL_DIM
GDN_CONV_WIDTH = 4
SC_CONV_WIDTH = 3
CARRY_ROWS = 8
DELTA_BLOCK = 128
V7X_VMEM_BYTES = 64 * 1024 * 1024
VMEM_LIMIT_BYTES = V7X_VMEM_BYTES - 8 * 1024 * 1024


def _dot(a, b):
    return jnp.dot(a, b, preferred_element_type=F32)


def _rms(x, g):
    return x * lax.rsqrt(jnp.mean(x * x, axis=-1, keepdims=True) + RMS_EPS) * g


def _sigmoid(x):
    return 1.0 / (1.0 + jnp.exp(-x))


def _silu(x):
    return x * _sigmoid(x)


def _softplus(x):
    return jnp.maximum(x, 0.0) + jnp.log(1.0 + jnp.exp(-jnp.abs(x)))


def _resident(shape):
    return pl.BlockSpec(shape, lambda *_: (0,) * len(shape), pipeline_mode=pl.Buffered(1))


def _params(n_grid_dims):
    return pltpu.CompilerParams(
        dimension_semantics=("arbitrary",) * n_grid_dims,
        vmem_limit_bytes=VMEM_LIMIT_BYTES,
    )


def _ffn_kernel(h_ref, gpre_ref, gpost_ref, wgu_ref, wd_ref, o_ref):
    d_ff = wd_ref.shape[0]
    x = h_ref[...]
    xn = _rms(x, gpre_ref[...]).astype(BF16)
    gate = _dot(xn, wgu_ref[:, :d_ff])
    up = _dot(xn, wgu_ref[:, d_ff:])
    act = (_silu(gate) * up).astype(BF16)
    ff = _dot(act, wd_ref[...])
    o_ref[...] = x + 0.5 * _rms(ff, gpost_ref[...])


def _ffn(h2d, g_pre, g_post, wgu, wd, tm):
    m, d = h2d.shape
    return pl.pallas_call(
        _ffn_kernel,
        grid=(m // tm,),
        in_specs=[
            pl.BlockSpec((tm, d), lambda i: (i, 0)),
            _resident((1, d)),
            _resident((1, d)),
            _resident(wgu.shape),
            _resident(wd.shape),
        ],
        out_specs=pl.BlockSpec((tm, d), lambda i: (i, 0)),
        out_shape=jax.ShapeDtypeStruct((m, d), F32),
        compiler_params=_params(1),
        name="ffn",
    )(h2d, g_pre, g_post, wgu, wd)


def _sc_kernel(h_ref, gpre_ref, gpost_ref, win_ref, cw_ref, wout_ref, o_ref, ubuf):
    tt, d = h_ref.shape[1], h_ref.shape[2]

    @pl.when(pl.program_id(1) == 0)
    def _():
        ubuf[0:CARRY_ROWS, :] = jnp.zeros((CARRY_ROWS, d), F32)

    x = h_ref[0]
    xn = _rms(x, gpre_ref[...]).astype(BF16)
    proj = _dot(xn, win_ref[...])
    u = proj[:, d:2 * d] * proj[:, 2 * d:]
    ubuf[CARRY_ROWS:CARRY_ROWS + tt, :] = u
    cw = cw_ref[...]
    conv = cw[2:3] * u
    conv += cw[1:2] * ubuf[CARRY_ROWS - 1:CARRY_ROWS - 1 + tt, :]
    conv += cw[0:1] * ubuf[CARRY_ROWS - 2:CARRY_ROWS - 2 + tt, :]
    y = (proj[:, :d] * conv).astype(BF16)
    mix = _dot(y, wout_ref[...])
    o_ref[0] = x + _rms(mix, gpost_ref[...])
    ubuf[0:CARRY_ROWS, :] = ubuf[tt:tt + CARRY_ROWS, :]


def _short_conv(h, g_pre, g_post, w_in, conv_w, w_out, tt):
    b, t, d = h.shape
    return pl.pallas_call(
        _sc_kernel,
        grid=(b, t // tt),
        in_specs=[
            pl.BlockSpec((1, tt, d), lambda i, j: (i, j, 0)),
            _resident((1, d)),
            _resident((1, d)),
            _resident(w_in.shape),
            _resident(conv_w.shape),
            _resident(w_out.shape),
        ],
        out_specs=pl.BlockSpec((1, tt, d), lambda i, j: (i, j, 0)),
        out_shape=jax.ShapeDtypeStruct((b, t, d), F32),
        scratch_shapes=[pltpu.VMEM((tt + CARRY_ROWS, d), F32)],
        compiler_params=_params(2),
        name="short_conv",
    )(h, g_pre, g_post, w_in, conv_w, w_out)


def _gdn_proj_kernel(h_ref, gpre_ref, wqkv_ref, wz_ref, wba_ref, cw_ref, alog_ref, dtb_ref,
                     q_ref, k_ref, v_ref, sz_ref, bg_ref, bgt_ref, cbuf):
    tt = h_ref.shape[1]

    @pl.when(pl.program_id(1) == 0)
    def _():
        cbuf[0:CARRY_ROWS, :] = jnp.zeros((CARRY_ROWS, CONV_DIM), F32)

    xn = _rms(h_ref[0], gpre_ref[...]).astype(BF16)
    cw = cw_ref[...]
    for cb in range(CONV_DIM // KEY_DIM):
        cols = slice(cb * KEY_DIM, (cb + 1) * KEY_DIM)
        p = _dot(xn, wqkv_ref[:, cols])
        cbuf[CARRY_ROWS:CARRY_ROWS + tt, cols] = p
        conv = cw[3:4, cols] * p
        for back in range(1, GDN_CONV_WIDTH):
            r0 = CARRY_ROWS - back
            conv += cw[3 - back:4 - back, cols] * cbuf[r0:r0 + tt, cols]
        s = _silu(conv)
        for hd in range(K_HEADS):
            sh = s[:, hd * HEAD_DIM:(hd + 1) * HEAD_DIM]
            if cb < 2:
                sh = sh * lax.rsqrt(jnp.sum(sh * sh, axis=-1, keepdims=True) + L2_EPS)
            if cb == 0:
                q_ref[0, hd] = (sh * HEAD_DIM ** -0.5).astype(BF16)
            elif cb == 1:
                k_ref[0, hd] = sh.astype(BF16)
            else:
                v_ref[0, (cb - 2) * K_HEADS + hd] = sh.astype(BF16)
    cbuf[0:CARRY_ROWS, :] = cbuf[tt:tt + CARRY_ROWS, :]

    sz = _silu(_dot(xn, wz_ref[...]))
    for hd in range(V_HEADS):
        sz_ref[0, hd] = sz[:, hd * HEAD_DIM:(hd + 1) * HEAD_DIM].astype(BF16)

    ba = _dot(xn, wba_ref[...])
    lane = lax.broadcasted_iota(jnp.int32, (tt, LANES), 1)
    row = lax.broadcasted_iota(jnp.int32, (tt, LANES), 0) % DELTA_BLOCK
    is_g = (lane >= V_HEADS) & (lane < 2 * V_HEADS)
    g = jnp.where(is_g, -jnp.exp(alog_ref[...]) * _softplus(ba + dtb_ref[...]), 0.0)
    shift = 1
    while shift < DELTA_BLOCK:
        g = g + jnp.where(row >= shift, pltpu.roll(g, shift, 0), 0.0)
        shift *= 2
    bg = jnp.where(lane < V_HEADS, _sigmoid(ba), g)
    bg_ref[0] = bg
    bgt_ref[0] = bg.T


def _gdn_proj(h, g_pre, w_qkv, w_z, w_ba, conv_w, a_log, dt_bias, tt):
    b, t, d = h.shape
    head_major = lambda nh: pl.BlockSpec((1, nh, tt, HEAD_DIM), lambda i, j: (i, 0, j, 0))
    return pl.pallas_call(
        _gdn_proj_kernel,
        grid=(b, t // tt),
        in_specs=[
            pl.BlockSpec((1, tt, d), lambda i, j: (i, j, 0)),
            _resident((1, d)),
            _resident(w_qkv.shape),
            _resident(w_z.shape),
            _resident(w_ba.shape),
            _resident(conv_w.shape),
            _resident((1, LANES)),
            _resident((1, LANES)),
        ],
        out_specs=[
            head_major(K_HEADS),
            head_major(K_HEADS),
            head_major(V_HEADS),
            head_major(V_HEADS),
            pl.BlockSpec((1, tt, LANES), lambda i, j: (i, j, 0)),
            pl.BlockSpec((1, LANES, tt), lambda i, j: (i, 0, j)),
        ],
        out_shape=[
            jax.ShapeDtypeStruct((b, K_HEADS, t, HEAD_DIM), BF16),
            jax.ShapeDtypeStruct((b, K_HEADS, t, HEAD_DIM), BF16),
            jax.ShapeDtypeStruct((b, V_HEADS, t, HEAD_DIM), BF16),
            jax.ShapeDtypeStruct((b, V_HEADS, t, HEAD_DIM), BF16),
            jax.ShapeDtypeStruct((b, t, LANES), F32),
            jax.ShapeDtypeStruct((b, LANES, t), F32),
        ],
        scratch_shapes=[pltpu.VMEM((tt + CARRY_ROWS, CONV_DIM), F32)],
        compiler_params=_params(2),
        name="gdn_proj",
    )(h, g_pre, w_qkv, w_z, w_ba, conv_w, a_log, dt_bias)


def _bmm(a, b):
    return jnp.einsum("nij,njk->nik", a.astype(BF16), b.astype(BF16), preferred_element_type=F32)


def _unit_lower_inverse_minus_identity(a):
    c = a.shape[-1]
    ri = lax.broadcasted_iota(jnp.int32, (c, c), 0)
    ci = lax.broadcasted_iota(jnp.int32, (c, c), 1)

    def below_diagonal_blocks(s):
        return (ri // (2 * s) == ci // (2 * s)) & ((ri // s) % 2 == 1) & ((ci // s) % 2 == 0)

    n = -jnp.where(below_diagonal_blocks(1), a, 0.0)
    s = 2
    while s < c:
        a_s = jnp.where(below_diagonal_blocks(s), a, 0.0)
        y = a_s + _bmm(a_s, n)
        n = n - (y + _bmm(n, y))
        s *= 2
    return n


def _delta_kernel(q_ref, k_ref, v_ref, sz_ref, bg_ref, bgt_ref, nw_ref, o_ref,
                  s_ref, u_s, w_s, qg_s, kt_s, att_s, egl_s):
    c = DELTA_BLOCK
    tt = bg_ref.shape[1]
    nblk = tt // c

    @pl.when(pl.program_id(1) == 0)
    def _():
        s_ref[...] = jnp.zeros(s_ref.shape, F32)

    bg = bg_ref[0]
    lane = lax.broadcasted_iota(jnp.int32, (tt, LANES), 1)
    ri = lax.broadcasted_iota(jnp.int32, (c, c), 0)
    ci = lax.broadcasted_iota(jnp.int32, (c, c), 1)
    causal = ri >= ci
    strict = ri > ci

    def lane_column(idx):
        col = jnp.sum(jnp.where(lane == idx, bg, 0.0), axis=-1, keepdims=True)
        return col.reshape(nblk, c, 1)

    def key_head(j, carry):
        qj = q_ref[0, j].reshape(nblk, c, HEAD_DIM)
        kj = k_ref[0, j].reshape(nblk, c, HEAD_DIM)
        prod = jnp.einsum("ncd,nsd->ncs", jnp.concatenate([qj, kj], axis=1), kj,
                          preferred_element_type=F32)
        qf = qj.astype(F32)
        kf = kj.astype(F32)
        for hh in range(V_HEADS // K_HEADS):
            h = (V_HEADS // K_HEADS) * j + hh
            beta = lane_column(h)
            gcol = lane_column(V_HEADS + h)
            grow_all = bgt_ref[0, pl.ds(V_HEADS + h, 1), :]
            grow = jnp.stack([grow_all[:, n * c:(n + 1) * c] for n in range(nblk)])
            dec = jnp.exp(jnp.where(causal, gcol - grow, -jnp.inf))
            a = jnp.where(strict, beta * prod[:, c:] * dec, 0.0)
            n_inv = _unit_lower_inverse_minus_identity(a)
            glast = gcol[:, c - 1:c, :]
            eg = jnp.exp(gcol)
            vf = v_ref[0, h].reshape(nblk, c, HEAD_DIM).astype(F32)
            rhs = jnp.concatenate([beta * vf, (beta * eg) * kf], axis=-1)
            uw = rhs + _bmm(n_inv, rhs)
            att = jnp.where(causal, prod[:, :c] * dec, 0.0)
            u_s[h] = uw[:, :, :HEAD_DIM].reshape(tt, HEAD_DIM)
            w_s[h] = uw[:, :, HEAD_DIM:].reshape(tt, HEAD_DIM).astype(BF16)
            qg_s[h] = (qf * eg).reshape(tt, HEAD_DIM).astype(BF16)
            kt_s[h] = (kf * jnp.exp(glast - gcol)).reshape(tt, HEAD_DIM).astype(BF16)
            att_s[h] = att.reshape(tt, c).astype(BF16)
            egl_s[h] = jnp.broadcast_to(jnp.exp(glast), (nblk, 8, LANES)).reshape(nblk * 8, LANES)
        return carry

    lax.fori_loop(0, K_HEADS, key_head, 0)

    nw = nw_ref[...]

    def block_step(n, carry):
        r0 = pl.multiple_of(n * c, c)
        rows = pl.ds(r0, c)
        state = s_ref[...]
        state_b = state.astype(BF16)
        wq = jnp.concatenate([w_s[:, rows, :], qg_s[:, rows, :]], axis=1)
        r = jnp.einsum("hcd,hde->hce", wq, state_b, preferred_element_type=F32)
        v_new = (u_s[:, rows, :] - r[:, :c]).astype(BF16)
        o = r[:, c:] + jnp.einsum("hcs,hse->hce", att_s[:, rows, :], v_new,
                                  preferred_element_type=F32)
        decay = egl_s[:, pl.ds(pl.multiple_of(n * 8, 8), 8), :][:, 0:1, :]
        s_ref[...] = state * decay + jnp.einsum("hcd,hce->hde", kt_s[:, rows, :], v_new,
                                                preferred_element_type=F32)
        on = o * lax.rsqrt(jnp.mean(o * o, axis=-1, keepdims=True) + RMS_EPS) * nw
        on = on * sz_ref[0, :, rows, :].astype(F32)
        for h in range(V_HEADS):
            o_ref[0, rows, h * HEAD_DIM:(h + 1) * HEAD_DIM] = on[h].astype(BF16)
        return carry

    lax.fori_loop(0, nblk, block_step, 0)


def _delta_rule(q, k, v, sz, bg, bgt, norm_w, tt):
    b, _, t, _ = q.shape
    head_major = lambda nh: pl.BlockSpec((1, nh, tt, HEAD_DIM), lambda i, j: (i, 0, j, 0))
    per_head = lambda width, dtype: pltpu.VMEM((V_HEADS, tt, width), dtype)
    return pl.pallas_call(
        _delta_kernel,
        grid=(b, t // tt),
        in_specs=[
            head_major(K_HEADS),
            head_major(K_HEADS),
            head_major(V_HEADS),
            head_major(V_HEADS),
            pl.BlockSpec((1, tt, LANES), lambda i, j: (i, j, 0)),
            pl.BlockSpec((1, LANES, tt), lambda i, j: (i, 0, j)),
            _resident((1, HEAD_DIM)),
        ],
        out_specs=pl.BlockSpec((1, tt, VAL_DIM), lambda i, j: (i, j, 0)),
        out_shape=jax.ShapeDtypeStruct((b, t, VAL_DIM), BF16),
        scratch_shapes=[
            pltpu.VMEM((V_HEADS, HEAD_DIM, HEAD_DIM), F32),
            per_head(HEAD_DIM, F32),
            per_head(HEAD_DIM, BF16),
            per_head(HEAD_DIM, BF16),
            per_head(HEAD_DIM, BF16),
            per_head(DELTA_BLOCK, BF16),
            pltpu.VMEM((V_HEADS, 8 * (tt // DELTA_BLOCK), LANES), F32),
        ],
        compiler_params=_params(2),
        name="delta_rule",
    )(q, k, v, sz, bg, bgt, norm_w)


def _out_proj_kernel(h_ref, y_ref, gpost_ref, w_ref, o_ref):
    mix = _dot(y_ref[...], w_ref[...])
    o_ref[...] = h_ref[...] + _rms(mix, gpost_ref[...])


def _out_proj(h2d, y2d, g_post, w_out, tm):
    m, d = h2d.shape
    return pl.pallas_call(
        _out_proj_kernel,
        grid=(m // tm,),
        in_specs=[
            pl.BlockSpec((tm, d), lambda i: (i, 0)),
            pl.BlockSpec((tm, y2d.shape[1]), lambda i: (i, 0)),
            _resident((1, d)),
            _resident(w_out.shape),
        ],
        out_specs=pl.BlockSpec((tm, d), lambda i: (i, 0)),
        out_shape=jax.ShapeDtypeStruct((m, d), F32),
        compiler_params=_params(1),
        name="gdn_out_proj",
    )(h2d, y2d, g_post, w_out)


def _pad_lanes(vec, offset):
    return jnp.zeros((1, LANES), F32).at[0, offset:offset + vec.shape[0]].set(vec.astype(F32))


def _gated_deltanet(h, g_pre, g_post, w_in, conv_w, a_log, dt_bias, norm_w, w_out, tt, tm):
    b, t, d = h.shape
    w_qkv = w_in[:, :CONV_DIM].astype(BF16)
    w_z = w_in[:, CONV_DIM:CONV_DIM + VAL_DIM].astype(BF16)
    w_ba = jnp.zeros((d, LANES), F32).at[:, :2 * V_HEADS].set(w_in[:, CONV_DIM + VAL_DIM:])
    q, k, v, sz, bg, bgt = _gdn_proj(
        h, g_pre, w_qkv, w_z, w_ba.astype(BF16), conv_w.astype(F32),
        _pad_lanes(a_log, V_HEADS), _pad_lanes(dt_bias, V_HEADS), tt)
    y = _delta_rule(q, k, v, sz, bg, bgt, norm_w.astype(F32).reshape(1, HEAD_DIM), tt)
    out = _out_proj(h.reshape(b * t, d), y.reshape(b * t, VAL_DIM), g_post,
                    w_out.astype(BF16), tm)
    return out.reshape(b, t, d)


def _tile(n, want):
    return want if n % want == 0 else n


def kernel(x, norm_g, ffn_w_gate_up, ffn_w_down, sc_w_in, sc_conv_w, sc_w_out, gdn_w_in,
           gdn_conv_w, gdn_a_log, gdn_dt_bias, gdn_norm_w, gdn_w_out):
    b, t, d = x.shape
    depth = norm_g.shape[0]
    tm = _tile(b * t, 512)
    tt = _tile(t, 512)
    gains = norm_g.astype(F32).reshape(depth, norm_g.shape[1], 1, d)
    h = x.astype(F32)
    for i in range(depth):
        g = gains[i]
        h = _ffn(h.reshape(b * t, d), g[0], g[1], ffn_w_gate_up[i, 0].astype(BF16),
                 ffn_w_down[i, 0].astype(BF16), tm).reshape(b, t, d)
        j = i // 2
        if i % 2 == 0:
            h = _short_conv(h, g[2], g[3], sc_w_in[j].astype(BF16), sc_conv_w[j].astype(F32),
                            sc_w_out[j].astype(BF16), tt)
        else:
            h = _gated_deltanet(h, g[2], g[3], gdn_w_in[j], gdn_conv_w[j], gdn_a_log[j],
                                gdn_dt_bias[j], gdn_norm_w[j], gdn_w_out[j], tt, tm)
        h = _ffn(h.reshape(b * t, d), g[4], g[5], ffn_w_gate_up[i, 1].astype(BF16),
                 ffn_w_down[i, 1].astype(BF16), tm).reshape(b, t, d)
    return h
```

```python
import functools

import jax
import jax.numpy as jnp
from jax import lax
from jax.experimental import pallas as pl
from jax.experimental.pallas import tpu as pltpu

F32 = jnp.float32
BF16 = jnp.bfloat16

RMS_EPS = 1e-6
L2_EPS = 1e-6
LANES = 128
HEAD_DIM = 128
K_HEADS = 8
V_HEADS = 16
KEY_DIM = K_HEADS * HEAD_DIM
VAL_DIM = V_HEADS * HEAD_DIM
CONV_DIM = 2 * KEY_DIM + VAL_DIM
GDN_CONV_WIDTH = 4
SC_CONV_WIDTH = 3
CARRY_ROWS = 8
DELTA_BLOCK = 128
KEY_HEADS_PER_STEP = 8
DELTA_TIME_TILE = 256
ROW_TILE = 512
FFN_ROW_TILE = 1024
FFN_SUB_ROWS = 256
V7X_VMEM_BYTES = 64 * 1024 * 1024
VMEM_LIMIT_BYTES = V7X_VMEM_BYTES - 8 * 1024 * 1024


def _dot(a, b):
    return jnp.dot(a, b, preferred_element_type=F32)


def _rms(x, g):
    return x * lax.rsqrt(jnp.mean(x * x, axis=-1, keepdims=True) + RMS_EPS) * g


def _sigmoid(x):
    return 1.0 / (1.0 + jnp.exp(-x))


def _silu(x):
    return x * _sigmoid(x)


def _softplus(x):
    return jnp.maximum(x, 0.0) + jnp.log(1.0 + jnp.exp(-jnp.abs(x)))


def _resident(shape):
    return pl.BlockSpec(shape, lambda *_: (0,) * len(shape), pipeline_mode=pl.Buffered(1))


def _params(n_grid_dims):
    return pltpu.CompilerParams(
        dimension_semantics=("arbitrary",) * n_grid_dims,
        vmem_limit_bytes=VMEM_LIMIT_BYTES,
    )


def _ffn_kernel(h_ref, gpre_ref, gpost_ref, wgu_ref, wd_ref, o_ref):
    d_ff = wd_ref.shape[0]
    for r0 in range(0, h_ref.shape[0], FFN_SUB_ROWS):
        rows = slice(r0, r0 + FFN_SUB_ROWS)
        x = h_ref[rows, :]
        xn = _rms(x, gpre_ref[...]).astype(BF16)
        gate = _dot(xn, wgu_ref[:, :d_ff])
        up = _dot(xn, wgu_ref[:, d_ff:])
        act = (_silu(gate) * up).astype(BF16)
        ff = _dot(act, wd_ref[...])
        o_ref[rows, :] = x + 0.5 * _rms(ff, gpost_ref[...])


def _ffn(h2d, g_pre, g_post, wgu, wd, tm):
    m, d = h2d.shape
    return pl.pallas_call(
        _ffn_kernel,
        grid=(m // tm,),
        in_specs=[
            pl.BlockSpec((tm, d), lambda i: (i, 0)),
            _resident((1, d)),
            _resident((1, d)),
            _resident(wgu.shape),
            _resident(wd.shape),
        ],
        out_specs=pl.BlockSpec((tm, d), lambda i: (i, 0)),
        out_shape=jax.ShapeDtypeStruct((m, d), F32),
        compiler_params=_params(1),
        name="ffn",
    )(h2d, g_pre, g_post, wgu, wd)


def _sc_kernel(h_ref, gpre_ref, gpost_ref, win_ref, cw_ref, wout_ref, o_ref, ubuf):
    tt, d = h_ref.shape[1], h_ref.shape[2]

    @pl.when(pl.program_id(1) == 0)
    def _():
        ubuf[0:CARRY_ROWS, :] = jnp.zeros((CARRY_ROWS, d), F32)

    x = h_ref[0]
    xn = _rms(x, gpre_ref[...]).astype(BF16)
    proj = _dot(xn, win_ref[...])
    u = proj[:, d:2 * d] * proj[:, 2 * d:]
    ubuf[CARRY_ROWS:CARRY_ROWS + tt, :] = u
    cw = cw_ref[...]
    conv = cw[2:3] * u
    conv += cw[1:2] * ubuf[CARRY_ROWS - 1:CARRY_ROWS - 1 + tt, :]
    conv += cw[0:1] * ubuf[CARRY_ROWS - 2:CARRY_ROWS - 2 + tt, :]
    y = (proj[:, :d] * conv).astype(BF16)
    mix = _dot(y, wout_ref[...])
    o_ref[0] = x + _rms(mix, gpost_ref[...])
    ubuf[0:CARRY_ROWS, :] = ubuf[tt:tt + CARRY_ROWS, :]


def _short_conv(h, g_pre, g_post, w_in, conv_w, w_out, tt):
    b, t, d = h.shape
    return pl.pallas_call(
        _sc_kernel,
        grid=(b, t // tt),
        in_specs=[
            pl.BlockSpec((1, tt, d), lambda i, j: (i, j, 0)),
            _resident((1, d)),
            _resident((1, d)),
            _resident(w_in.shape),
            _resident(conv_w.shape),
            _resident(w_out.shape),
        ],
        out_specs=pl.BlockSpec((1, tt, d), lambda i, j: (i, j, 0)),
        out_shape=jax.ShapeDtypeStruct((b, t, d), F32),
        scratch_shapes=[pltpu.VMEM((tt + CARRY_ROWS, d), F32)],
        compiler_params=_params(2),
        name="short_conv",
    )(h, g_pre, g_post, w_in, conv_w, w_out)


def _gdn_proj_kernel(h_ref, gpre_ref, wqkv_ref, wz_ref, wba_ref, cw_ref, alog_ref, dtb_ref,
                     q_ref, k_ref, v_ref, sz_ref, bg_ref, bgt_ref, cbuf):
    tt = h_ref.shape[1]

    @pl.when(pl.program_id(1) == 0)
    def _():
        cbuf[0:CARRY_ROWS, :] = jnp.zeros((CARRY_ROWS, CONV_DIM), F32)

    xn = _rms(h_ref[0], gpre_ref[...]).astype(BF16)
    cw = cw_ref[...]
    for cb in range(CONV_DIM // KEY_DIM):
        cols = slice(cb * KEY_DIM, (cb + 1) * KEY_DIM)
        p = _dot(xn, wqkv_ref[:, cols])
        cbuf[CARRY_ROWS:CARRY_ROWS + tt, cols] = p
        conv = cw[3:4, cols] * p
        for back in range(1, GDN_CONV_WIDTH):
            r0 = CARRY_ROWS - back
            conv += cw[3 - back:4 - back, cols] * cbuf[r0:r0 + tt, cols]
        s = _silu(conv)
        for hd in range(K_HEADS):
            sh = s[:, hd * HEAD_DIM:(hd + 1) * HEAD_DIM]
            if cb < 2:
                sh = sh * lax.rsqrt(jnp.sum(sh * sh, axis=-1, keepdims=True) + L2_EPS)
            if cb == 0:
                q_ref[0, hd] = (sh * HEAD_DIM ** -0.5).astype(BF16)
            elif cb == 1:
                k_ref[0, hd] = sh.astype(BF16)
            else:
                v_ref[0, (cb - 2) * K_HEADS + hd] = sh.astype(BF16)
    cbuf[0:CARRY_ROWS, :] = cbuf[tt:tt + CARRY_ROWS, :]

    sz = _silu(_dot(xn, wz_ref[...]))
    for hd in range(V_HEADS):
        sz_ref[0, hd] = sz[:, hd * HEAD_DIM:(hd + 1) * HEAD_DIM].astype(BF16)

    ba = _dot(xn, wba_ref[...])
    lane = lax.broadcasted_iota(jnp.int32, (tt, LANES), 1)
    row = lax.broadcasted_iota(jnp.int32, (tt, LANES), 0) % DELTA_BLOCK
    is_g = (lane >= V_HEADS) & (lane < 2 * V_HEADS)
    g = jnp.where(is_g, -jnp.exp(alog_ref[...]) * _softplus(ba + dtb_ref[...]), 0.0)
    shift = 1
    while shift < DELTA_BLOCK:
        g = g + jnp.where(row >= shift, pltpu.roll(g, shift, 0), 0.0)
        shift *= 2
    bg = jnp.where(lane < V_HEADS, _sigmoid(ba), g)
    bg_ref[0] = bg
    bgt_ref[0] = bg.T


def _gdn_proj(h, g_pre, w_qkv, w_z, w_ba, conv_w, a_log, dt_bias, tt):
    b, t, d = h.shape
    head_major = lambda nh: pl.BlockSpec((1, nh, tt, HEAD_DIM), lambda i, j: (i, 0, j, 0))
    return pl.pallas_call(
        _gdn_proj_kernel,
        grid=(b, t // tt),
        in_specs=[
            pl.BlockSpec((1, tt, d), lambda i, j: (i, j, 0)),
            _resident((1, d)),
            _resident(w_qkv.shape),
            _resident(w_z.shape),
            _resident(w_ba.shape),
            _resident(conv_w.shape),
            _resident((1, LANES)),
            _resident((1, LANES)),
        ],
        out_specs=[
            head_major(K_HEADS),
            head_major(K_HEADS),
            head_major(V_HEADS),
            head_major(V_HEADS),
            pl.BlockSpec((1, tt, LANES), lambda i, j: (i, j, 0)),
            pl.BlockSpec((1, LANES, tt), lambda i, j: (i, 0, j)),
        ],
        out_shape=[
            jax.ShapeDtypeStruct((b, K_HEADS, t, HEAD_DIM), BF16),
            jax.ShapeDtypeStruct((b, K_HEADS, t, HEAD_DIM), BF16),
            jax.ShapeDtypeStruct((b, V_HEADS, t, HEAD_DIM), BF16),
            jax.ShapeDtypeStruct((b, V_HEADS, t, HEAD_DIM), BF16),
            jax.ShapeDtypeStruct((b, t, LANES), F32),
            jax.ShapeDtypeStruct((b, LANES, t), F32),
        ],
        scratch_shapes=[pltpu.VMEM((tt + CARRY_ROWS, CONV_DIM), F32)],
        compiler_params=_params(2),
        name="gdn_proj",
    )(h, g_pre, w_qkv, w_z, w_ba, conv_w, a_log, dt_bias)


def _bmm(a, b):
    return jnp.einsum("nij,njk->nik", a.astype(BF16), b.astype(BF16), preferred_element_type=F32)


def _level_masks(c):
    ri = lax.broadcasted_iota(jnp.int32, (c, c), 0)
    ci = lax.broadcasted_iota(jnp.int32, (c, c), 1)
    x = ri ^ ci
    masks = []
    s = 1
    while s < c:
        masks.append(((ri > ci) & (x >= s) & (x < 2 * s)).astype(BF16))
        s *= 2
    return masks


def _unit_lower_inverse(a, masks):
    c = a.shape[-1]
    eye = (lax.broadcasted_iota(jnp.int32, (c, c), 0)
           == lax.broadcasted_iota(jnp.int32, (c, c), 1)).astype(F32)
    d = eye - (a * masks[0]).astype(F32)
    for m in masks[1:]:
        y = _bmm(a * m, d)
        d = d - _bmm(d, y)
    return d


def _delta_kernel(q_ref, k_ref, v_ref, sz_ref, bg_ref, bgt_ref, nw_ref, o_ref,
                  s_ref, u_s, w_s, qg_s, kt_s, att_s, egl_s):
    c = DELTA_BLOCK
    tt = bg_ref.shape[1]
    nblk = tt // c

    @pl.when(pl.program_id(1) == 0)
    def _():
        s_ref[...] = jnp.zeros(s_ref.shape, F32)

    bg = bg_ref[0]
    lane = lax.broadcasted_iota(jnp.int32, (tt, LANES), 1)
    ri = lax.broadcasted_iota(jnp.int32, (c, c), 0)
    ci = lax.broadcasted_iota(jnp.int32, (c, c), 1)
    causal = ri >= ci
    masks = _level_masks(c)

    def lane_column(idx):
        col = jnp.sum(jnp.where(lane == idx, bg, 0.0), axis=-1, keepdims=True)
        return col.reshape(nblk, c, 1)

    rep = V_HEADS // K_HEADS

    grp = KEY_HEADS_PER_STEP
    nch = grp * rep * nblk

    def key_heads(j, carry):
        k0 = grp * j
        h0 = rep * k0
        heads = pl.ds(h0, rep * grp)
        qj = q_ref[0, pl.ds(k0, grp)].reshape(grp * nblk, c, HEAD_DIM)
        kj = k_ref[0, pl.ds(k0, grp)].reshape(grp * nblk, c, HEAD_DIM)
        prod = jnp.einsum("ncd,nsd->ncs", jnp.concatenate([qj, kj], axis=1), kj,
                          preferred_element_type=F32)
        per_value_head = lambda x: jnp.concatenate(
            [x[g * nblk:(g + 1) * nblk] for g in range(grp) for _ in range(rep)], axis=0)
        qk = per_value_head(prod[:, :c])
        kk = per_value_head(prod[:, c:])
        qf = per_value_head(qj.astype(F32))
        kf = per_value_head(kj.astype(F32))
        spread = lambda idx: jnp.broadcast_to(
            jnp.concatenate([lane_column(idx + hh) for hh in range(rep * grp)], axis=0),
            (nch, c, LANES))
        beta = spread(h0)
        gcol = spread(V_HEADS + h0)
        grow_all = [bgt_ref[0, pl.ds(V_HEADS + h0 + hh, 1), :] for hh in range(rep * grp)]
        grow = jnp.stack([grow_all[hh][:, n * c:(n + 1) * c]
                          for hh in range(rep * grp) for n in range(nblk)])
        dec = jnp.exp(jnp.where(causal, gcol - grow, -jnp.inf))
        t_inv = _unit_lower_inverse((beta * kk * dec).astype(BF16), masks)
        glast = gcol[:, c - 1:c, :]
        eg = jnp.exp(gcol)
        vf = v_ref[0, heads].reshape(nch, c, HEAD_DIM).astype(F32)
        rhs = jnp.concatenate([beta * vf, (beta * eg) * kf], axis=-1)
        uw = _bmm(t_inv, rhs)
        per_head = lambda x: x.reshape(rep * grp, tt, x.shape[-1])
        u_s[heads] = per_head(uw[:, :, :HEAD_DIM])
        w_s[heads] = per_head(uw[:, :, HEAD_DIM:]).astype(BF16)
        qg_s[heads] = per_head(qf * eg).astype(BF16)
        kt_s[heads] = per_head(kf * jnp.exp(glast - gcol)).astype(BF16)
        att_s[heads] = per_head(qk * dec).astype(BF16)
        egl_s[heads] = jnp.broadcast_to(jnp.exp(glast), (nch, 8, LANES)).reshape(
            rep * grp, nblk * 8, LANES)
        return carry

    lax.fori_loop(0, K_HEADS // grp, key_heads, 0)

    nw = nw_ref[...]

    def block_step(n, carry):
        r0 = pl.multiple_of(n * c, c)
        rows = pl.ds(r0, c)
        state = s_ref[...]
        state_b = state.astype(BF16)
        wq = jnp.concatenate([w_s[:, rows, :], qg_s[:, rows, :]], axis=1)
        r = jnp.einsum("hcd,hde->hce", wq, state_b, preferred_element_type=F32)
        v_new = (u_s[:, rows, :] - r[:, :c]).astype(BF16)
        o = r[:, c:] + jnp.einsum("hcs,hse->hce", att_s[:, rows, :], v_new,
                                  preferred_element_type=F32)
        decay = egl_s[:, pl.ds(pl.multiple_of(n * 8, 8), 8), :][:, 0:1, :]
        s_ref[...] = state * decay + jnp.einsum("hcd,hce->hde", kt_s[:, rows, :], v_new,
                                                preferred_element_type=F32)
        on = o * lax.rsqrt(jnp.mean(o * o, axis=-1, keepdims=True) + RMS_EPS) * nw
        on = on * sz_ref[0, :, rows, :].astype(F32)
        for h in range(V_HEADS):
            o_ref[0, rows, h * HEAD_DIM:(h + 1) * HEAD_DIM] = on[h].astype(BF16)
        return carry

    lax.fori_loop(0, nblk, block_step, 0)


def _delta_rule(q, k, v, sz, bg, bgt, norm_w, tt):
    b, _, t, _ = q.shape
    head_major = lambda nh: pl.BlockSpec((1, nh, tt, HEAD_DIM), lambda i, j: (i, 0, j, 0))
    per_head = lambda width, dtype: pltpu.VMEM((V_HEADS, tt, width), dtype)
    return pl.pallas_call(
        _delta_kernel,
        grid=(b, t // tt),
        in_specs=[
            head_major(K_HEADS),
            head_major(K_HEADS),
            head_major(V_HEADS),
            head_major(V_HEADS),
            pl.BlockSpec((1, tt, LANES), lambda i, j: (i, j, 0)),
            pl.BlockSpec((1, LANES, tt), lambda i, j: (i, 0, j)),
            _resident((1, HEAD_DIM)),
        ],
        out_specs=pl.BlockSpec((1, tt, VAL_DIM), lambda i, j: (i, j, 0)),
        out_shape=jax.ShapeDtypeStruct((b, t, VAL_DIM), BF16),
        scratch_shapes=[
            pltpu.VMEM((V_HEADS, HEAD_DIM, HEAD_DIM), F32),
            per_head(HEAD_DIM, F32),
            per_head(HEAD_DIM, BF16),
            per_head(HEAD_DIM, BF16),
            per_head(HEAD_DIM, BF16),
            per_head(DELTA_BLOCK, BF16),
            pltpu.VMEM((V_HEADS, 8 * (tt // DELTA_BLOCK), LANES), F32),
        ],
        compiler_params=_params(2),
        name="delta_rule",
    )(q, k, v, sz, bg, bgt, norm_w)


def _out_proj_kernel(h_ref, y_ref, gpost_ref, w_ref, o_ref):
    mix = _dot(y_ref[...], w_ref[...])
    o_ref[...] = h_ref[...] + _rms(mix, gpost_ref[...])


def _out_proj(h2d, y2d, g_post, w_out, tm):
    m, d = h2d.shape
    return pl.pallas_call(
        _out_proj_kernel,
        grid=(m // tm,),
        in_specs=[
            pl.BlockSpec((tm, d), lambda i: (i, 0)),
            pl.BlockSpec((tm, y2d.shape[1]), lambda i: (i, 0)),
            _resident((1, d)),
            _resident(w_out.shape),
        ],
        out_specs=pl.BlockSpec((tm, d), lambda i: (i, 0)),
        out_shape=jax.ShapeDtypeStruct((m, d), F32),
        compiler_params=_params(1),
        name="gdn_out_proj",
    )(h2d, y2d, g_post, w_out)


def _pad_lanes(vec, offset):
    return jnp.zeros((1, LANES), F32).at[0, offset:offset + vec.shape[0]].set(vec.astype(F32))


def _gated_deltanet(h, g_pre, g_post, w_in, conv_w, a_log, dt_bias, norm_w, w_out, tt, tm):
    b, t, d = h.shape
    w_qkv = w_in[:, :CONV_DIM].astype(BF16)
    w_z = w_in[:, CONV_DIM:CONV_DIM + VAL_DIM].astype(BF16)
    w_ba = jnp.zeros((d, LANES), F32).at[:, :2 * V_HEADS].set(w_in[:, CONV_DIM + VAL_DIM:])
    q, k, v, sz, bg, bgt = _gdn_proj(
        h, g_pre, w_qkv, w_z, w_ba.astype(BF16), conv_w.astype(F32),
        _pad_lanes(a_log, V_HEADS), _pad_lanes(dt_bias, V_HEADS), tt)
    y = _delta_rule(q, k, v, sz, bg, bgt, norm_w.astype(F32).reshape(1, HEAD_DIM),
                    _tile(t, DELTA_TIME_TILE))
    out = _out_proj(h.reshape(b * t, d), y.reshape(b * t, VAL_DIM), g_post,
                    w_out.astype(BF16), tm)
    return out.reshape(b, t, d)


def _tile(n, want):
    return want if n % want == 0 else n


def kernel(x, norm_g, ffn_w_gate_up, ffn_w_down, sc_w_in, sc_conv_w, sc_w_out, gdn_w_in,
           gdn_conv_w, gdn_a_log, gdn_dt_bias, gdn_norm_w, gdn_w_out):
    b, t, d = x.shape
    depth = norm_g.shape[0]
    tm = _tile(b * t, ROW_TILE)
    tt = _tile(t, ROW_TILE)
    tf = _tile(b * t, FFN_ROW_TILE)
    gains = norm_g.astype(F32).reshape(depth, norm_g.shape[1], 1, d)
    h = x.astype(F32)
    for i in range(depth):
        g = gains[i]
        h = _ffn(h.reshape(b * t, d), g[0], g[1], ffn_w_gate_up[i, 0].astype(BF16),
                 ffn_w_down[i, 0].astype(BF16), tf).reshape(b, t, d)
        j = i // 2
        if i % 2 == 0:
            h = _short_conv(h, g[2], g[3], sc_w_in[j].astype(BF16), sc_conv_w[j].astype(F32),
                            sc_w_out[j].astype(BF16), tt)
        else:
            h = _gated_deltanet(h, g[2], g[3], gdn_w_in[j], gdn_conv_w[j], gdn_a_log[j],
                                gdn_dt_bias[j], gdn_norm_w[j], gdn_w_out[j], tt, tm)
        h = _ffn(h.reshape(b * t, d), g[4], g[5], ffn_w_gate_up[i, 1].astype(BF16),
                 ffn_w_down[i, 1].astype(BF16), tf).reshape(b, t, d)
    return h
```

```python
import functools

import jax
import jax.numpy as jnp
from jax import lax
from jax.experimental import pallas as pl
from jax.experimental.pallas import tpu as pltpu

F32 = jnp.float32
BF16 = jnp.bfloat16

RMS_EPS = 1e-6
L2_EPS = 1e-6
LANES = 128
HEAD_DIM = 128
K_HEADS = 8
V_HEADS = 16
KEY_DIM = K_HEADS * HEAD_DIM
VAL_DIM = V_HEADS * HEAD_DIM
CONV_DIM = 2 * KEY_DIM + VAL_DIM
GDN_CONV_WIDTH = 4
SC_CONV_WIDTH = 3
CARRY_ROWS = 8
DELTA_BLOCK = 128
DELTA_TIME_TILE = 256
ROW_TILE = 512
FFN_ROW_TILE = 1024
FFN_SUB_ROWS = 128
V7X_VMEM_BYTES = 64 * 1024 * 1024
VMEM_LIMIT_BYTES = V7X_VMEM_BYTES - 8 * 1024 * 1024


def _dot(a, b):
    return jnp.dot(a, b, preferred_element_type=F32)


def _rms(x, g):
    return x * lax.rsqrt(jnp.mean(x * x, axis=-1, keepdims=True) + RMS_EPS) * g


def _sigmoid(x):
    return 1.0 / (1.0 + jnp.exp(-x))


def _silu(x):
    return x * _sigmoid(x)


def _softplus(x):
    return jnp.maximum(x, 0.0) + jnp.log(1.0 + jnp.exp(-jnp.abs(x)))


def _causal_conv(p, taps, carry_ref, cols):
    tt = p.shape[0]
    first_row = lax.broadcasted_iota(jnp.int32, (CARRY_ROWS, p.shape[1]), 0) == 0
    r = taps[0:1] * p
    for j in range(1, taps.shape[0]):
        prev = carry_ref[j - 1, :, cols]
        carry_ref[j - 1, :, cols] = r[tt - CARRY_ROWS:]
        rolled = pltpu.roll(r, 1, 0)
        head = jnp.where(first_row, prev[CARRY_ROWS - 1:], rolled[:CARRY_ROWS])
        r = taps[j:j + 1] * p + jnp.concatenate([head, rolled[CARRY_ROWS:]], axis=0)
    return r


def _resident(shape):
    return pl.BlockSpec(shape, lambda *_: (0,) * len(shape), pipeline_mode=pl.Buffered(1))


def _params(n_grid_dims):
    return pltpu.CompilerParams(
        dimension_semantics=("arbitrary",) * n_grid_dims,
        vmem_limit_bytes=VMEM_LIMIT_BYTES,
    )


def _ffn_kernel(h_ref, gpre_ref, gpost_ref, wgu_ref, wd_ref, o_ref):
    d_ff = wd_ref.shape[0]
    for r0 in range(0, h_ref.shape[0], FFN_SUB_ROWS):
        rows = slice(r0, r0 + FFN_SUB_ROWS)
        x = h_ref[rows, :]
        xn = _rms(x, gpre_ref[...]).astype(BF16)
        gate = _dot(xn, wgu_ref[:, :d_ff])
        up = _dot(xn, wgu_ref[:, d_ff:])
        act = (_silu(gate) * up).astype(BF16)
        ff = _dot(act, wd_ref[...])
        o_ref[rows, :] = x + 0.5 * _rms(ff, gpost_ref[...])


def _ffn(h2d, g_pre, g_post, wgu, wd, tm):
    m, d = h2d.shape
    return pl.pallas_call(
        _ffn_kernel,
        grid=(m // tm,),
        in_specs=[
            pl.BlockSpec((tm, d), lambda i: (i, 0)),
            _resident((1, d)),
            _resident((1, d)),
            _resident(wgu.shape),
            _resident(wd.shape),
        ],
        out_specs=pl.BlockSpec((tm, d), lambda i: (i, 0)),
        out_shape=jax.ShapeDtypeStruct((m, d), F32),
        compiler_params=_params(1),
        name="ffn",
    )(h2d, g_pre, g_post, wgu, wd)


def _sc_kernel(h_ref, gpre_ref, gpost_ref, win_ref, cw_ref, wout_ref, o_ref, carry):
    d = h_ref.shape[2]

    @pl.when(pl.program_id(1) == 0)
    def _():
        carry[...] = jnp.zeros(carry.shape, F32)

    x = h_ref[0]
    xn = _rms(x, gpre_ref[...]).astype(BF16)
    proj = _dot(xn, win_ref[...])
    u = proj[:, d:2 * d] * proj[:, 2 * d:]
    conv = _causal_conv(u, cw_ref[...], carry, slice(None))
    y = (proj[:, :d] * conv).astype(BF16)
    mix = _dot(y, wout_ref[...])
    o_ref[0] = x + _rms(mix, gpost_ref[...])


def _short_conv(h, g_pre, g_post, w_in, conv_w, w_out, tt):
    b, t, d = h.shape
    return pl.pallas_call(
        _sc_kernel,
        grid=(b, t // tt),
        in_specs=[
            pl.BlockSpec((1, tt, d), lambda i, j: (i, j, 0)),
            _resident((1, d)),
            _resident((1, d)),
            _resident(w_in.shape),
            _resident(conv_w.shape),
            _resident(w_out.shape),
        ],
        out_specs=pl.BlockSpec((1, tt, d), lambda i, j: (i, j, 0)),
        out_shape=jax.ShapeDtypeStruct((b, t, d), F32),
        scratch_shapes=[pltpu.VMEM((SC_CONV_WIDTH - 1, CARRY_ROWS, d), F32)],
        compiler_params=_params(2),
        name="short_conv",
    )(h, g_pre, g_post, w_in, conv_w, w_out)


def _gdn_proj_kernel(h_ref, gpre_ref, wqkv_ref, wz_ref, wba_ref, cw_ref, alog_ref, dtb_ref,
                     q_ref, k_ref, v_ref, z_ref, bg_ref, bgt_ref, cbuf):
    tt = h_ref.shape[1]

    @pl.when(pl.program_id(1) == 0)
    def _():
        cbuf[...] = jnp.zeros(cbuf.shape, F32)

    lane = lax.broadcasted_iota(jnp.int32, (DELTA_BLOCK, LANES), 1)
    row = lax.broadcasted_iota(jnp.int32, (DELTA_BLOCK, LANES), 0)
    is_g = (lane >= V_HEADS) & (lane < 2 * V_HEADS)
    for r0 in range(0, tt, DELTA_BLOCK):
        rows = slice(r0, r0 + DELTA_BLOCK)
        xn = _rms(h_ref[0, rows, :], gpre_ref[...]).astype(BF16)
        for cb in range(CONV_DIM // KEY_DIM):
            cols = slice(cb * KEY_DIM, (cb + 1) * KEY_DIM)
            p = _dot(xn, wqkv_ref[:, cols])
            s = _silu(_causal_conv(p, cw_ref[:, cols], cbuf, cols))
            for hd in range(K_HEADS):
                sh = s[:, hd * HEAD_DIM:(hd + 1) * HEAD_DIM]
                if cb < 2:
                    sh = sh * lax.rsqrt(jnp.sum(sh * sh, axis=-1, keepdims=True) + L2_EPS)
                if cb == 0:
                    q_ref[0, hd, rows, :] = (sh * HEAD_DIM ** -0.5).astype(BF16)
                elif cb == 1:
                    k_ref[0, hd, rows, :] = sh.astype(BF16)
                else:
                    v_ref[0, (cb - 2) * K_HEADS + hd, rows, :] = sh.astype(BF16)

        z = _dot(xn, wz_ref[...])
        for hd in range(V_HEADS):
            z_ref[0, hd, rows, :] = z[:, hd * HEAD_DIM:(hd + 1) * HEAD_DIM].astype(BF16)

        ba = _dot(xn, wba_ref[...])
        g = jnp.where(is_g, -jnp.exp(alog_ref[...]) * _softplus(ba + dtb_ref[...]), 0.0)
        shift = 1
        while shift < DELTA_BLOCK:
            g = g + jnp.where(row >= shift, pltpu.roll(g, shift, 0), 0.0)
            shift *= 2
        bg = jnp.where(lane < V_HEADS, _sigmoid(ba), g)
        bg_ref[0, rows, :] = bg
        bgt_ref[0, :, rows] = bg.T


def _gdn_proj(h, g_pre, w_qkv, w_z, w_ba, conv_w, a_log, dt_bias, tt):
    b, t, d = h.shape
    head_major = lambda nh: pl.BlockSpec((1, nh, tt, HEAD_DIM), lambda i, j: (i, 0, j, 0))
    return pl.pallas_call(
        _gdn_proj_kernel,
        grid=(b, t // tt),
        in_specs=[
            pl.BlockSpec((1, tt, d), lambda i, j: (i, j, 0)),
            _resident((1, d)),
            _resident(w_qkv.shape),
            _resident(w_z.shape),
            _resident(w_ba.shape),
            _resident(conv_w.shape),
            _resident((1, LANES)),
            _resident((1, LANES)),
        ],
        out_specs=[
            head_major(K_HEADS),
            head_major(K_HEADS),
            head_major(V_HEADS),
            head_major(V_HEADS),
            pl.BlockSpec((1, tt, LANES), lambda i, j: (i, j, 0)),
            pl.BlockSpec((1, LANES, tt), lambda i, j: (i, 0, j)),
        ],
        out_shape=[
            jax.ShapeDtypeStruct((b, K_HEADS, t, HEAD_DIM), BF16),
            jax.ShapeDtypeStruct((b, K_HEADS, t, HEAD_DIM), BF16),
            jax.ShapeDtypeStruct((b, V_HEADS, t, HEAD_DIM), BF16),
            jax.ShapeDtypeStruct((b, V_HEADS, t, HEAD_DIM), BF16),
            jax.ShapeDtypeStruct((b, t, LANES), F32),
            jax.ShapeDtypeStruct((b, LANES, t), F32),
        ],
        scratch_shapes=[pltpu.VMEM((GDN_CONV_WIDTH - 1, CARRY_ROWS, CONV_DIM), F32)],
        compiler_params=_params(2),
        name="gdn_proj",
    )(h, g_pre, w_qkv, w_z, w_ba, conv_w, a_log, dt_bias)


def _bmm(a, b):
    return jnp.einsum("nij,njk->nik", a.astype(BF16), b.astype(BF16), preferred_element_type=F32)


def _level_masks(c):
    ri = lax.broadcasted_iota(jnp.int32, (c, c), 0)
    ci = lax.broadcasted_iota(jnp.int32, (c, c), 1)
    x = ri ^ ci
    masks = []
    s = 1
    while s < c:
        masks.append(((ri > ci) & (x >= s) & (x < 2 * s)).astype(BF16))
        s *= 2
    return masks


def _unit_lower_inverse(a, masks):
    c = a.shape[-1]
    eye = (lax.broadcasted_iota(jnp.int32, (c, c), 0)
           == lax.broadcasted_iota(jnp.int32, (c, c), 1)).astype(F32)
    d = eye - (a * masks[0]).astype(F32)
    for m in masks[1:]:
        y = _bmm(a * m, d)
        d = d - _bmm(d, y)
    return d


def _delta_kernel(q_ref, k_ref, v_ref, z_ref, bg_ref, bgt_ref, nw_ref, h_ref, gpost_ref, wout_ref,
                  o_ref, s_ref, u_s, w_s, qg_s, kt_s, att_s, egl_s):
    c = DELTA_BLOCK
    tt = bg_ref.shape[1]
    nblk = tt // c

    @pl.when(pl.program_id(1) == 0)
    def _():
        s_ref[...] = jnp.zeros(s_ref.shape, F32)

    ri = lax.broadcasted_iota(jnp.int32, (c, c), 0)
    ci = lax.broadcasted_iota(jnp.int32, (c, c), 1)
    causal = ri >= ci
    rep = V_HEADS // K_HEADS
    nch = V_HEADS * nblk

    qj = q_ref[0].reshape(K_HEADS * nblk, c, HEAD_DIM)
    kj = k_ref[0].reshape(K_HEADS * nblk, c, HEAD_DIM)
    prod = jnp.einsum("ncd,nsd->ncs", jnp.concatenate([qj, kj], axis=1), kj,
                      preferred_element_type=F32)
    per_value_head = lambda x: jnp.concatenate(
        [x[g * nblk:(g + 1) * nblk] for g in range(K_HEADS) for _ in range(rep)], axis=0)
    qk = per_value_head(prod[:, :c])
    kk = per_value_head(prod[:, c:])
    qf = per_value_head(qj.astype(F32))
    kf = per_value_head(kj.astype(F32))
    bg = bg_ref[0]
    spread = lambda lane0: jnp.broadcast_to(
        jnp.concatenate([bg[:, lane0 + h:lane0 + h + 1].reshape(nblk, c, 1)
                         for h in range(V_HEADS)], axis=0), (nch, c, LANES))
    beta = spread(0)
    gcol = spread(V_HEADS)
    grow = jnp.stack([bgt_ref[0, V_HEADS + h:V_HEADS + h + 1, n * c:(n + 1) * c]
                      for h in range(V_HEADS) for n in range(nblk)])
    dec = jnp.exp(jnp.where(causal, gcol - grow, -jnp.inf))
    t_inv = _unit_lower_inverse((beta * kk * dec).astype(BF16), _level_masks(c))
    glast = gcol[:, c - 1:c, :]
    eg = jnp.exp(gcol)
    vf = v_ref[0].reshape(nch, c, HEAD_DIM).astype(F32)
    rhs = jnp.concatenate([beta * vf, (beta * eg) * kf], axis=-1)
    uw = _bmm(t_inv, rhs)
    per_head = lambda x: x.reshape(V_HEADS, tt, x.shape[-1])
    u_s[...] = per_head(uw[:, :, :HEAD_DIM])
    w_s[...] = per_head(uw[:, :, HEAD_DIM:]).astype(BF16)
    qg_s[...] = per_head(qf * eg).astype(BF16)
    kt_s[...] = per_head(kf * jnp.exp(glast - gcol)).astype(BF16)
    att_s[...] = per_head(qk * dec).astype(BF16)
    egl_s[...] = jnp.broadcast_to(jnp.exp(glast), (nch, 8, LANES)).reshape(
        V_HEADS, nblk * 8, LANES)

    nw = nw_ref[...]
    state = s_ref[...]
    for n in range(nblk):
        rows = slice(n * c, (n + 1) * c)
        wq = jnp.concatenate([w_s[:, rows, :], qg_s[:, rows, :]], axis=1)
        r = jnp.einsum("hcd,hde->hce", wq, state.astype(BF16), preferred_element_type=F32)
        v_new = (u_s[:, rows, :] - r[:, :c]).astype(BF16)
        o = r[:, c:] + jnp.einsum("hcs,hse->hce", att_s[:, rows, :], v_new,
                                  preferred_element_type=F32)
        decay = egl_s[:, n * 8:n * 8 + 1, :]
        state = state * decay + jnp.einsum("hcd,hce->hde", kt_s[:, rows, :], v_new,
                                           preferred_element_type=F32)
        on = o * lax.rsqrt(jnp.mean(o * o, axis=-1, keepdims=True) + RMS_EPS) * nw
        y = on * _silu(z_ref[0, :, rows, :].astype(F32))
        y = jnp.concatenate([y[h].astype(BF16) for h in range(V_HEADS)], axis=-1)
        mix = _dot(y, wout_ref[...])
        o_ref[0, rows, :] = h_ref[0, rows, :] + _rms(mix, gpost_ref[...])
    s_ref[...] = state


def _delta_rule(h, q, k, v, z, bg, bgt, norm_w, g_post, w_out, tt):
    b, t, d = h.shape
    head_major = lambda nh: pl.BlockSpec((1, nh, tt, HEAD_DIM), lambda i, j: (i, 0, j, 0))
    per_head = lambda width, dtype: pltpu.VMEM((V_HEADS, tt, width), dtype)
    return pl.pallas_call(
        _delta_kernel,
        grid=(b, t // tt),
        in_specs=[
            head_major(K_HEADS),
            head_major(K_HEADS),
            head_major(V_HEADS),
            head_major(V_HEADS),
            pl.BlockSpec((1, tt, LANES), lambda i, j: (i, j, 0)),
            pl.BlockSpec((1, LANES, tt), lambda i, j: (i, 0, j)),
            _resident((1, HEAD_DIM)),
            pl.BlockSpec((1, tt, d), lambda i, j: (i, j, 0)),
            _resident((1, d)),
            _resident(w_out.shape),
        ],
        out_specs=pl.BlockSpec((1, tt, d), lambda i, j: (i, j, 0)),
        out_shape=jax.ShapeDtypeStruct((b, t, d), F32),
        scratch_shapes=[
            pltpu.VMEM((V_HEADS, HEAD_DIM, HEAD_DIM), F32),
            per_head(HEAD_DIM, F32),
            per_head(HEAD_DIM, BF16),
            per_head(HEAD_DIM, BF16),
            per_head(HEAD_DIM, BF16),
            per_head(DELTA_BLOCK, BF16),
            pltpu.VMEM((V_HEADS, 8 * (tt // DELTA_BLOCK), LANES), F32),
        ],
        compiler_params=_params(2),
        name="delta_rule",
    )(q, k, v, z, bg, bgt, norm_w, h, g_post, w_out)


def _pad_lanes(vec, offset):
    return jnp.zeros((1, LANES), F32).at[0, offset:offset + vec.shape[0]].set(vec.astype(F32))


def _gated_deltanet(h, g_pre, g_post, w_in, conv_w, a_log, dt_bias, norm_w, w_out, tt):
    d = h.shape[2]
    w_qkv = w_in[:, :CONV_DIM].astype(BF16)
    w_z = w_in[:, CONV_DIM:CONV_DIM + VAL_DIM].astype(BF16)
    w_ba = jnp.zeros((d, LANES), F32).at[:, :2 * V_HEADS].set(w_in[:, CONV_DIM + VAL_DIM:])
    q, k, v, z, bg, bgt = _gdn_proj(
        h, g_pre, w_qkv, w_z, w_ba.astype(BF16), conv_w.astype(F32),
        _pad_lanes(a_log, V_HEADS), _pad_lanes(dt_bias, V_HEADS), tt)
    return _delta_rule(h, q, k, v, z, bg, bgt, norm_w.astype(F32).reshape(1, HEAD_DIM), g_post,
                       w_out.astype(BF16), _tile(h.shape[1], DELTA_TIME_TILE))


def _tile(n, want):
    return want if n % want == 0 else n


def kernel(x, norm_g, ffn_w_gate_up, ffn_w_down, sc_w_in, sc_conv_w, sc_w_out, gdn_w_in,
           gdn_conv_w, gdn_a_log, gdn_dt_bias, gdn_norm_w, gdn_w_out):
    b, t, d = x.shape
    depth = norm_g.shape[0]
    tt = _tile(t, ROW_TILE)
    tf = _tile(b * t, FFN_ROW_TILE)
    gains = norm_g.astype(F32).reshape(depth, norm_g.shape[1], 1, d)
    h = x.astype(F32)
    for i in range(depth):
        g = gains[i]
        h = _ffn(h.reshape(b * t, d), g[0], g[1], ffn_w_gate_up[i, 0].astype(BF16),
                 ffn_w_down[i, 0].astype(BF16), tf).reshape(b, t, d)
        j = i // 2
        if i % 2 == 0:
            h = _short_conv(h, g[2], g[3], sc_w_in[j].astype(BF16), sc_conv_w[j].astype(F32),
                            sc_w_out[j].astype(BF16), tt)
        else:
            h = _gated_deltanet(h, g[2], g[3], gdn_w_in[j], gdn_conv_w[j], gdn_a_log[j],
                                gdn_dt_bias[j], gdn_norm_w[j], gdn_w_out[j], tt)
        h = _ffn(h.reshape(b * t, d), g[4], g[5], ffn_w_gate_up[i, 1].astype(BF16),
                 ffn_w_down[i, 1].astype(BF16), tf).reshape(b, t, d)
    return h
```

```python
import functools

import jax
import jax.numpy as jnp
from jax import lax
from jax.experimental import pallas as pl
from jax.experimental.pallas import tpu as pltpu

F32 = jnp.float32
BF16 = jnp.bfloat16

RMS_EPS = 1e-6
L2_EPS = 1e-6
LANES = 128
HEAD_DIM = 128
K_HEADS = 8
V_HEADS = 16
KEY_DIM = K_HEADS * HEAD_DIM
VAL_DIM = V_HEADS * HEAD_DIM
CONV_DIM = 2 * KEY_DIM + VAL_DIM
GDN_CONV_WIDTH = 4
SC_CONV_WIDTH = 3
CARRY_ROWS = 8
DELTA_BLOCK = 128
DELTA_TIME_TILE = 256
ROW_TILE = 512
FFN_ROW_TILE = 1024
FFN_SUB_ROWS = 256
GDN_PROJ_SUB_ROWS = 512
V7X_VMEM_BYTES = 64 * 1024 * 1024
VMEM_LIMIT_BYTES = V7X_VMEM_BYTES - 8 * 1024 * 1024


def _dot(a, b):
    return jnp.dot(a, b, preferred_element_type=F32)


def _rms(x, g):
    return x * lax.rsqrt(jnp.mean(x * x, axis=-1, keepdims=True) + RMS_EPS) * g


def _sigmoid(x):
    return 1.0 / (1.0 + jnp.exp(-x))


def _silu(x):
    return x * _sigmoid(x)


def _softplus(x):
    return jnp.maximum(x, 0.0) + jnp.log(1.0 + jnp.exp(-jnp.abs(x)))


def _causal_conv(p, taps, hist_ref, cols):
    tt, k = p.shape[0], taps.shape[0]
    hist_ref[CARRY_ROWS:CARRY_ROWS + tt, cols] = p
    conv = taps[k - 1:k] * p
    for back in range(1, k):
        r0 = CARRY_ROWS - back
        conv += taps[k - 1 - back:k - back] * hist_ref[r0:r0 + tt, cols]
    hist_ref[0:CARRY_ROWS, cols] = hist_ref[tt:tt + CARRY_ROWS, cols]
    return conv


def _resident(shape):
    return pl.BlockSpec(shape, lambda *_: (0,) * len(shape), pipeline_mode=pl.Buffered(1))


def _params(n_grid_dims):
    return pltpu.CompilerParams(
        dimension_semantics=("arbitrary",) * n_grid_dims,
        vmem_limit_bytes=VMEM_LIMIT_BYTES,
    )


def _ffn_kernel(h_ref, gpre_ref, gpost_ref, wgu_ref, wd_ref, o_ref):
    d_ff = wd_ref.shape[0]
    for r0 in range(0, h_ref.shape[0], FFN_SUB_ROWS):
        rows = slice(r0, r0 + FFN_SUB_ROWS)
        x = h_ref[rows, :]
        xn = _rms(x, gpre_ref[...]).astype(BF16)
        gate = _dot(xn, wgu_ref[:, :d_ff])
        up = _dot(xn, wgu_ref[:, d_ff:])
        act = (_silu(gate) * up).astype(BF16)
        ff = _dot(act, wd_ref[...])
        o_ref[rows, :] = x + 0.5 * _rms(ff, gpost_ref[...])


def _ffn(h2d, g_pre, g_post, wgu, wd, tm):
    m, d = h2d.shape
    return pl.pallas_call(
        _ffn_kernel,
        grid=(m // tm,),
        in_specs=[
            pl.BlockSpec((tm, d), lambda i: (i, 0)),
            _resident((1, d)),
            _resident((1, d)),
            _resident(wgu.shape),
            _resident(wd.shape),
        ],
        out_specs=pl.BlockSpec((tm, d), lambda i: (i, 0)),
        out_shape=jax.ShapeDtypeStruct((m, d), F32),
        compiler_params=_params(1),
        name="ffn",
    )(h2d, g_pre, g_post, wgu, wd)


def _sc_kernel(h_ref, gpre_ref, gpost_ref, win_ref, cw_ref, wout_ref, o_ref, carry):
    d = h_ref.shape[2]

    @pl.when(pl.program_id(1) == 0)
    def _():
        carry[0:CARRY_ROWS, :] = jnp.zeros((CARRY_ROWS, d), F32)

    x = h_ref[0]
    xn = _rms(x, gpre_ref[...]).astype(BF16)
    proj = _dot(xn, win_ref[...])
    u = proj[:, d:2 * d] * proj[:, 2 * d:]
    conv = _causal_conv(u, cw_ref[...], carry, slice(None))
    y = (proj[:, :d] * conv).astype(BF16)
    mix = _dot(y, wout_ref[...])
    o_ref[0] = x + _rms(mix, gpost_ref[...])


def _short_conv(h, g_pre, g_post, w_in, conv_w, w_out, tt):
    b, t, d = h.shape
    return pl.pallas_call(
        _sc_kernel,
        grid=(b, t // tt),
        in_specs=[
            pl.BlockSpec((1, tt, d), lambda i, j: (i, j, 0)),
            _resident((1, d)),
            _resident((1, d)),
            _resident(w_in.shape),
            _resident(conv_w.shape),
            _resident(w_out.shape),
        ],
        out_specs=pl.BlockSpec((1, tt, d), lambda i, j: (i, j, 0)),
        out_shape=jax.ShapeDtypeStruct((b, t, d), F32),
        scratch_shapes=[pltpu.VMEM((tt + CARRY_ROWS, d), F32)],
        compiler_params=_params(2),
        name="short_conv",
    )(h, g_pre, g_post, w_in, conv_w, w_out)


def _gdn_proj_kernel(h_ref, gpre_ref, wqkv_ref, wz_ref, wba_ref, cw_ref, alog_ref, dtb_ref,
                     q_ref, k_ref, v_ref, z_ref, bg_ref, bgt_ref, cbuf):
    tt = h_ref.shape[1]

    @pl.when(pl.program_id(1) == 0)
    def _():
        cbuf[0:CARRY_ROWS, :] = jnp.zeros((CARRY_ROWS, CONV_DIM), F32)

    sub = min(tt, GDN_PROJ_SUB_ROWS)
    lane = lax.broadcasted_iota(jnp.int32, (sub, LANES), 1)
    row = lax.broadcasted_iota(jnp.int32, (sub, LANES), 0) % DELTA_BLOCK
    is_g = (lane >= V_HEADS) & (lane < 2 * V_HEADS)
    for r0 in range(0, tt, sub):
        rows = slice(r0, r0 + sub)
        xn = _rms(h_ref[0, rows, :], gpre_ref[...]).astype(BF16)
        for cb in range(CONV_DIM // KEY_DIM):
            cols = slice(cb * KEY_DIM, (cb + 1) * KEY_DIM)
            p = _dot(xn, wqkv_ref[:, cols])
            s = _silu(_causal_conv(p, cw_ref[:, cols], cbuf, cols))
            for hd in range(K_HEADS):
                sh = s[:, hd * HEAD_DIM:(hd + 1) * HEAD_DIM]
                if cb < 2:
                    sh = sh * lax.rsqrt(jnp.sum(sh * sh, axis=-1, keepdims=True) + L2_EPS)
                if cb == 0:
                    q_ref[0, hd, rows, :] = (sh * HEAD_DIM ** -0.5).astype(BF16)
                elif cb == 1:
                    k_ref[0, hd, rows, :] = sh.astype(BF16)
                else:
                    v_ref[0, (cb - 2) * K_HEADS + hd, rows, :] = sh.astype(BF16)

        z = _dot(xn, wz_ref[...])
        for hd in range(V_HEADS):
            z_ref[0, hd, rows, :] = z[:, hd * HEAD_DIM:(hd + 1) * HEAD_DIM].astype(BF16)

        ba = _dot(xn, wba_ref[...])
        g = jnp.where(is_g, -jnp.exp(alog_ref[...]) * _softplus(ba + dtb_ref[...]), 0.0)
        shift = 1
        while shift < DELTA_BLOCK:
            g = g + jnp.where(row >= shift, pltpu.roll(g, shift, 0), 0.0)
            shift *= 2
        bg = jnp.where(lane < V_HEADS, _sigmoid(ba), g)
        bg_ref[0, rows, :] = bg
        bgt_ref[0, :, rows] = bg.T


def _gdn_proj(h, g_pre, w_qkv, w_z, w_ba, conv_w, a_log, dt_bias, tt):
    b, t, d = h.shape
    head_major = lambda nh: pl.BlockSpec((1, nh, tt, HEAD_DIM), lambda i, j: (i, 0, j, 0))
    return pl.pallas_call(
        _gdn_proj_kernel,
        grid=(b, t // tt),
        in_specs=[
            pl.BlockSpec((1, tt, d), lambda i, j: (i, j, 0)),
            _resident((1, d)),
            _resident(w_qkv.shape),
            _resident(w_z.shape),
            _resident(w_ba.shape),
            _resident(conv_w.shape),
            _resident((1, LANES)),
            _resident((1, LANES)),
        ],
        out_specs=[
            head_major(K_HEADS),
            head_major(K_HEADS),
            head_major(V_HEADS),
            head_major(V_HEADS),
            pl.BlockSpec((1, tt, LANES), lambda i, j: (i, j, 0)),
            pl.BlockSpec((1, LANES, tt), lambda i, j: (i, 0, j)),
        ],
        out_shape=[
            jax.ShapeDtypeStruct((b, K_HEADS, t, HEAD_DIM), BF16),
            jax.ShapeDtypeStruct((b, K_HEADS, t, HEAD_DIM), BF16),
            jax.ShapeDtypeStruct((b, V_HEADS, t, HEAD_DIM), BF16),
            jax.ShapeDtypeStruct((b, V_HEADS, t, HEAD_DIM), BF16),
            jax.ShapeDtypeStruct((b, t, LANES), F32),
            jax.ShapeDtypeStruct((b, LANES, t), F32),
        ],
        scratch_shapes=[pltpu.VMEM((tt + CARRY_ROWS, CONV_DIM), F32)],
        compiler_params=_params(2),
        name="gdn_proj",
    )(h, g_pre, w_qkv, w_z, w_ba, conv_w, a_log, dt_bias)


def _bmm(a, b):
    return jnp.einsum("nij,njk->nik", a.astype(BF16), b.astype(BF16), preferred_element_type=F32)


def _level_masks(c):
    ri = lax.broadcasted_iota(jnp.int32, (c, c), 0)
    ci = lax.broadcasted_iota(jnp.int32, (c, c), 1)
    x = ri ^ ci
    masks = []
    s = 1
    while s < c:
        masks.append(((ri > ci) & (x >= s) & (x < 2 * s)).astype(BF16))
        s *= 2
    return masks


def _unit_lower_inverse(a, masks):
    c = a.shape[-1]
    eye = (lax.broadcasted_iota(jnp.int32, (c, c), 0)
           == lax.broadcasted_iota(jnp.int32, (c, c), 1)).astype(F32)
    d = eye - (a * masks[0]).astype(F32)
    for m in masks[1:]:
        y = _bmm(a * m, d)
        d = d - _bmm(d, y)
    return d


def _delta_kernel(q_ref, k_ref, v_ref, z_ref, bg_ref, bgt_ref, nw_ref, h_ref, gpost_ref, wout_ref,
                  o_ref, s_ref, u_s, w_s, qg_s, kt_s, att_s, egl_s):
    c = DELTA_BLOCK
    tt = bg_ref.shape[1]
    nblk = tt // c

    @pl.when(pl.program_id(1) == 0)
    def _():
        s_ref[...] = jnp.zeros(s_ref.shape, F32)

    ri = lax.broadcasted_iota(jnp.int32, (c, c), 0)
    ci = lax.broadcasted_iota(jnp.int32, (c, c), 1)
    causal = ri >= ci
    rep = V_HEADS // K_HEADS
    nch = V_HEADS * nblk

    qj = q_ref[0].reshape(K_HEADS * nblk, c, HEAD_DIM)
    kj = k_ref[0].reshape(K_HEADS * nblk, c, HEAD_DIM)
    prod = jnp.einsum("ncd,nsd->ncs", jnp.concatenate([qj, kj], axis=1), kj,
                      preferred_element_type=F32)
    per_value_head = lambda x: jnp.concatenate(
        [x[g * nblk:(g + 1) * nblk] for g in range(K_HEADS) for _ in range(rep)], axis=0)
    qk = per_value_head(prod[:, :c])
    kk = per_value_head(prod[:, c:])
    qf = per_value_head(qj.astype(F32))
    kf = per_value_head(kj.astype(F32))
    bg = bg_ref[0]
    spread = lambda lane0: jnp.broadcast_to(
        jnp.concatenate([bg[:, lane0 + h:lane0 + h + 1].reshape(nblk, c, 1)
                         for h in range(V_HEADS)], axis=0), (nch, c, LANES))
    beta = spread(0)
    gcol = spread(V_HEADS)
    grow = jnp.stack([bgt_ref[0, V_HEADS + h:V_HEADS + h + 1, n * c:(n + 1) * c]
                      for h in range(V_HEADS) for n in range(nblk)])
    dec = jnp.exp(jnp.where(causal, gcol - grow, -jnp.inf))
    t_inv = _unit_lower_inverse((beta * kk * dec).astype(BF16), _level_masks(c))
    glast = gcol[:, c - 1:c, :]
    eg = jnp.exp(gcol)
    vf = v_ref[0].reshape(nch, c, HEAD_DIM).astype(F32)
    rhs = jnp.concatenate([beta * vf, (beta * eg) * kf], axis=-1)
    uw = _bmm(t_inv, rhs)
    per_head = lambda x: x.reshape(V_HEADS, tt, x.shape[-1])
    u_s[...] = per_head(uw[:, :, :HEAD_DIM])
    w_s[...] = per_head(uw[:, :, HEAD_DIM:]).astype(BF16)
    qg_s[...] = per_head(qf * eg).astype(BF16)
    kt_s[...] = per_head(kf * jnp.exp(glast - gcol)).astype(BF16)
    att_s[...] = per_head(qk * dec).astype(BF16)
    egl_s[...] = jnp.broadcast_to(jnp.exp(glast), (nch, 8, LANES)).reshape(
        V_HEADS, nblk * 8, LANES)

    nw = nw_ref[...]
    state = s_ref[...]
    for n in range(nblk):
        rows = slice(n * c, (n + 1) * c)
        wq = jnp.concatenate([w_s[:, rows, :], qg_s[:, rows, :]], axis=1)
        r = jnp.einsum("hcd,hde->hce", wq, state.astype(BF16), preferred_element_type=F32)
        v_new = (u_s[:, rows, :] - r[:, :c]).astype(BF16)
        o = r[:, c:] + jnp.einsum("hcs,hse->hce", att_s[:, rows, :], v_new,
                                  preferred_element_type=F32)
        decay = egl_s[:, n * 8:n * 8 + 1, :]
        state = state * decay + jnp.einsum("hcd,hce->hde", kt_s[:, rows, :], v_new,
                                           preferred_element_type=F32)
        on = o * lax.rsqrt(jnp.mean(o * o, axis=-1, keepdims=True) + RMS_EPS) * nw
        y = on * _silu(z_ref[0, :, rows, :].astype(F32))
        y = jnp.concatenate([y[h].astype(BF16) for h in range(V_HEADS)], axis=-1)
        mix = _dot(y, wout_ref[...])
        o_ref[0, rows, :] = h_ref[0, rows, :] + _rms(mix, gpost_ref[...])
    s_ref[...] = state


def _delta_rule(h, q, k, v, z, bg, bgt, norm_w, g_post, w_out, tt):
    b, t, d = h.shape
    head_major = lambda nh: pl.BlockSpec((1, nh, tt, HEAD_DIM), lambda i, j: (i, 0, j, 0))
    per_head = lambda width, dtype: pltpu.VMEM((V_HEADS, tt, width), dtype)
    return pl.pallas_call(
        _delta_kernel,
        grid=(b, t // tt),
        in_specs=[
            head_major(K_HEADS),
            head_major(K_HEADS),
            head_major(V_HEADS),
            head_major(V_HEADS),
            pl.BlockSpec((1, tt, LANES), lambda i, j: (i, j, 0)),
            pl.BlockSpec((1, LANES, tt), lambda i, j: (i, 0, j)),
            _resident((1, HEAD_DIM)),
            pl.BlockSpec((1, tt, d), lambda i, j: (i, j, 0)),
            _resident((1, d)),
            _resident(w_out.shape),
        ],
        out_specs=pl.BlockSpec((1, tt, d), lambda i, j: (i, j, 0)),
        out_shape=jax.ShapeDtypeStruct((b, t, d), F32),
        scratch_shapes=[
            pltpu.VMEM((V_HEADS, HEAD_DIM, HEAD_DIM), F32),
            per_head(HEAD_DIM, F32),
            per_head(HEAD_DIM, BF16),
            per_head(HEAD_DIM, BF16),
            per_head(HEAD_DIM, BF16),
            per_head(DELTA_BLOCK, BF16),
            pltpu.VMEM((V_HEADS, 8 * (tt // DELTA_BLOCK), LANES), F32),
        ],
        compiler_params=_params(2),
        name="delta_rule",
    )(q, k, v, z, bg, bgt, norm_w, h, g_post, w_out)


def _pad_lanes(vec, offset):
    return jnp.zeros((1, LANES), F32).at[0, offset:offset + vec.shape[0]].set(vec.astype(F32))


def _gated_deltanet(h, g_pre, g_post, w_in, conv_w, a_log, dt_bias, norm_w, w_out, tt):
    d = h.shape[2]
    w_qkv = w_in[:, :CONV_DIM].astype(BF16)
    w_z = w_in[:, CONV_DIM:CONV_DIM + VAL_DIM].astype(BF16)
    w_ba = jnp.zeros((d, LANES), F32).at[:, :2 * V_HEADS].set(w_in[:, CONV_DIM + VAL_DIM:])
    q, k, v, z, bg, bgt = _gdn_proj(
        h, g_pre, w_qkv, w_z, w_ba.astype(BF16), conv_w.astype(F32),
        _pad_lanes(a_log, V_HEADS), _pad_lanes(dt_bias, V_HEADS), tt)
    return _delta_rule(h, q, k, v, z, bg, bgt, norm_w.astype(F32).reshape(1, HEAD_DIM), g_post,
                       w_out.astype(BF16), _tile(h.shape[1], DELTA_TIME_TILE))


def _tile(n, want):
    return want if n % want == 0 else n


def kernel(x, norm_g, ffn_w_gate_up, ffn_w_down, sc_w_in, sc_conv_w, sc_w_out, gdn_w_in,
           gdn_conv_w, gdn_a_log, gdn_dt_bias, gdn_norm_w, gdn_w_out):
    b, t, d = x.shape
    depth = norm_g.shape[0]
    tt = _tile(t, ROW_TILE)
    tf = _tile(b * t, FFN_ROW_TILE)
    gains = norm_g.astype(F32).reshape(depth, norm_g.shape[1], 1, d)
    h = x.astype(F32)
    for i in range(depth):
        g = gains[i]
        h = _ffn(h.reshape(b * t, d), g[0], g[1], ffn_w_gate_up[i, 0].astype(BF16),
                 ffn_w_down[i, 0].astype(BF16), tf).reshape(b, t, d)
        j = i // 2
        if i % 2 == 0:
            h = _short_conv(h, g[2], g[3], sc_w_in[j].astype(BF16), sc_conv_w[j].astype(F32),
                            sc_w_out[j].astype(BF16), tt)
        else:
            h = _gated_deltanet(h, g[2], g[3], gdn_w_in[j], gdn_conv_w[j], gdn_a_log[j],
                                gdn_dt_bias[j], gdn_norm_w[j], gdn_w_out[j], tt)
        h = _ffn(h.reshape(b * t, d), g[4], g[5], ffn_w_gate_up[i, 1].astype(BF16),
                 ffn_w_down[i, 1].astype(BF16), tf).reshape(b, t, d)
    return h
```

```python
import functools

import jax
import jax.numpy as jnp
from jax import lax
from jax.experimental import pallas as pl
from jax.experimental.pallas import tpu as pltpu

F32 = jnp.float32
BF16 = jnp.bfloat16

RMS_EPS = 1e-6
L2_EPS = 1e-6
LANES = 128
HEAD_DIM = 128
K_HEADS = 8
V_HEADS = 16
KEY_DIM = K_HEADS * HEAD_DIM
VAL_DIM = V_HEADS * HEAD_DIM
CONV_DIM = 2 * KEY_DIM + VAL_DIM
GDN_CONV_WIDTH = 4
SC_CONV_WIDTH = 3
CARRY_ROWS = 8
DELTA_BLOCK = 128
DELTA_TIME_TILE = 256
ROW_TILE = 512
FFN_ROW_TILE = 1024
FFN_SUB_ROWS = 256
GDN_PROJ_SUB_ROWS = 256
SC_ROW_TILE = 1024
SC_SUB_ROWS = 256
V7X_VMEM_BYTES = 64 * 1024 * 1024
VMEM_LIMIT_BYTES = V7X_VMEM_BYTES - 8 * 1024 * 1024


def _dot(a, b):
    return jnp.dot(a, b, preferred_element_type=F32)


def _rms(x, g):
    return x * lax.rsqrt(jnp.mean(x * x, axis=-1, keepdims=True) + RMS_EPS) * g


def _sigmoid(x):
    return 1.0 / (1.0 + jnp.exp(-x))


def _silu(x):
    return x * _sigmoid(x)


def _softplus(x):
    return jnp.maximum(x, 0.0) + jnp.log(1.0 + jnp.exp(-jnp.abs(x)))


def _causal_conv(p, taps, hist_ref, cols):
    tt, k = p.shape[0], taps.shape[0]
    hist_ref[CARRY_ROWS:CARRY_ROWS + tt, cols] = p
    conv = taps[k - 1:k] * p
    for back in range(1, k):
        r0 = CARRY_ROWS - back
        conv += taps[k - 1 - back:k - back] * hist_ref[r0:r0 + tt, cols]
    hist_ref[0:CARRY_ROWS, cols] = hist_ref[tt:tt + CARRY_ROWS, cols]
    return conv


def _resident(shape):
    return pl.BlockSpec(shape, lambda *_: (0,) * len(shape), pipeline_mode=pl.Buffered(1))


def _params(n_grid_dims):
    return pltpu.CompilerParams(
        dimension_semantics=("arbitrary",) * n_grid_dims,
        vmem_limit_bytes=VMEM_LIMIT_BYTES,
    )


def _ffn_kernel(h_ref, gpre_ref, gpost_ref, wgu_ref, wd_ref, o_ref):
    d_ff = wd_ref.shape[0]
    for r0 in range(0, h_ref.shape[0], FFN_SUB_ROWS):
        rows = slice(r0, r0 + FFN_SUB_ROWS)
        x = h_ref[rows, :]
        xn = _rms(x, gpre_ref[...]).astype(BF16)
        gate = _dot(xn, wgu_ref[:, :d_ff])
        up = _dot(xn, wgu_ref[:, d_ff:])
        act = (_silu(gate) * up).astype(BF16)
        ff = _dot(act, wd_ref[...])
        o_ref[rows, :] = x + 0.5 * _rms(ff, gpost_ref[...])


def _ffn(h2d, g_pre, g_post, wgu, wd, tm):
    m, d = h2d.shape
    return pl.pallas_call(
        _ffn_kernel,
        grid=(m // tm,),
        in_specs=[
            pl.BlockSpec((tm, d), lambda i: (i, 0)),
            _resident((1, d)),
            _resident((1, d)),
            _resident(wgu.shape),
            _resident(wd.shape),
        ],
        out_specs=pl.BlockSpec((tm, d), lambda i: (i, 0)),
        out_shape=jax.ShapeDtypeStruct((m, d), F32),
        compiler_params=_params(1),
        name="ffn",
    )(h2d, g_pre, g_post, wgu, wd)


def _sc_kernel(h_ref, gpre_ref, gpost_ref, win_ref, cw_ref, wout_ref, o_ref, carry):
    d = h_ref.shape[2]

    @pl.when(pl.program_id(1) == 0)
    def _():
        carry[0:CARRY_ROWS, :] = jnp.zeros((CARRY_ROWS, d), F32)

    sub = min(h_ref.shape[1], SC_SUB_ROWS)
    for r0 in range(0, h_ref.shape[1], sub):
        rows = slice(r0, r0 + sub)
        x = h_ref[0, rows, :]
        xn = _rms(x, gpre_ref[...]).astype(BF16)
        proj = _dot(xn, win_ref[...])
        u = proj[:, d:2 * d] * proj[:, 2 * d:]
        conv = _causal_conv(u, cw_ref[...], carry, slice(None))
        y = (proj[:, :d] * conv).astype(BF16)
        mix = _dot(y, wout_ref[...])
        o_ref[0, rows, :] = x + _rms(mix, gpost_ref[...])


def _short_conv(h, g_pre, g_post, w_in, conv_w, w_out, tt):
    b, t, d = h.shape
    return pl.pallas_call(
        _sc_kernel,
        grid=(b, t // tt),
        in_specs=[
            pl.BlockSpec((1, tt, d), lambda i, j: (i, j, 0)),
            _resident((1, d)),
            _resident((1, d)),
            _resident(w_in.shape),
            _resident(conv_w.shape),
            _resident(w_out.shape),
        ],
        out_specs=pl.BlockSpec((1, tt, d), lambda i, j: (i, j, 0)),
        out_shape=jax.ShapeDtypeStruct((b, t, d), F32),
        scratch_shapes=[pltpu.VMEM((min(tt, SC_SUB_ROWS) + CARRY_ROWS, d), F32)],
        compiler_params=_params(2),
        name="short_conv",
    )(h, g_pre, g_post, w_in, conv_w, w_out)


def _gdn_proj_kernel(h_ref, gpre_ref, wqkv_ref, wz_ref, wba_ref, cw_ref, alog_ref, dtb_ref,
                     q_ref, k_ref, v_ref, z_ref, bg_ref, bgt_ref, cbuf):
    tt = h_ref.shape[1]

    @pl.when(pl.program_id(1) == 0)
    def _():
        cbuf[0:CARRY_ROWS, :] = jnp.zeros((CARRY_ROWS, CONV_DIM), F32)

    sub = min(tt, GDN_PROJ_SUB_ROWS)
    lane = lax.broadcasted_iota(jnp.int32, (sub, LANES), 1)
    row = lax.broadcasted_iota(jnp.int32, (sub, LANES), 0) % DELTA_BLOCK
    is_g = (lane >= V_HEADS) & (lane < 2 * V_HEADS)
    for r0 in range(0, tt, sub):
        rows = slice(r0, r0 + sub)
        xn = _rms(h_ref[0, rows, :], gpre_ref[...]).astype(BF16)
        for cb in range(CONV_DIM // KEY_DIM):
            cols = slice(cb * KEY_DIM, (cb + 1) * KEY_DIM)
            p = _dot(xn, wqkv_ref[:, cols])
            s = _silu(_causal_conv(p, cw_ref[:, cols], cbuf, cols))
            for hd in range(K_HEADS):
                sh = s[:, hd * HEAD_DIM:(hd + 1) * HEAD_DIM]
                if cb < 2:
                    sh = sh * lax.rsqrt(jnp.sum(sh * sh, axis=-1, keepdims=True) + L2_EPS)
                if cb == 0:
                    q_ref[0, hd, rows, :] = (sh * HEAD_DIM ** -0.5).astype(BF16)
                elif cb == 1:
                    k_ref[0, hd, rows, :] = sh.astype(BF16)
                else:
                    v_ref[0, (cb - 2) * K_HEADS + hd, rows, :] = sh.astype(BF16)

        z = _dot(xn, wz_ref[...])
        for hd in range(V_HEADS):
            z_ref[0, hd, rows, :] = z[:, hd * HEAD_DIM:(hd + 1) * HEAD_DIM].astype(BF16)

        ba = _dot(xn, wba_ref[...])
        g = jnp.where(is_g, -jnp.exp(alog_ref[...]) * _softplus(ba + dtb_ref[...]), 0.0)
        shift = 1
        while shift < DELTA_BLOCK:
            g = g + jnp.where(row >= shift, pltpu.roll(g, shift, 0), 0.0)
            shift *= 2
        bg = jnp.where(lane < V_HEADS, _sigmoid(ba), g)
        bg_ref[0, rows, :] = bg
        bgt_ref[0, :, rows] = bg.T


def _gdn_proj(h, g_pre, w_qkv, w_z, w_ba, conv_w, a_log, dt_bias, tt):
    b, t, d = h.shape
    head_major = lambda nh: pl.BlockSpec((1, nh, tt, HEAD_DIM), lambda i, j: (i, 0, j, 0))
    return pl.pallas_call(
        _gdn_proj_kernel,
        grid=(b, t // tt),
        in_specs=[
            pl.BlockSpec((1, tt, d), lambda i, j: (i, j, 0)),
            _resident((1, d)),
            _resident(w_qkv.shape),
            _resident(w_z.shape),
            _resident(w_ba.shape),
            _resident(conv_w.shape),
            _resident((1, LANES)),
            _resident((1, LANES)),
        ],
        out_specs=[
            head_major(K_HEADS),
            head_major(K_HEADS),
            head_major(V_HEADS),
            head_major(V_HEADS),
            pl.BlockSpec((1, tt, LANES), lambda i, j: (i, j, 0)),
            pl.BlockSpec((1, LANES, tt), lambda i, j: (i, 0, j)),
        ],
        out_shape=[
            jax.ShapeDtypeStruct((b, K_HEADS, t, HEAD_DIM), BF16),
            jax.ShapeDtypeStruct((b, K_HEADS, t, HEAD_DIM), BF16),
            jax.ShapeDtypeStruct((b, V_HEADS, t, HEAD_DIM), BF16),
            jax.ShapeDtypeStruct((b, V_HEADS, t, HEAD_DIM), BF16),
            jax.ShapeDtypeStruct((b, t, LANES), F32),
            jax.ShapeDtypeStruct((b, LANES, t), F32),
        ],
        scratch_shapes=[pltpu.VMEM((min(tt, GDN_PROJ_SUB_ROWS) + CARRY_ROWS, CONV_DIM), F32)],
        compiler_params=_params(2),
        name="gdn_proj",
    )(h, g_pre, w_qkv, w_z, w_ba, conv_w, a_log, dt_bias)


def _bmm(a, b):
    return jnp.einsum("nij,njk->nik", a.astype(BF16), b.astype(BF16), preferred_element_type=F32)


def _level_masks(c):
    ri = lax.broadcasted_iota(jnp.int32, (c, c), 0)
    ci = lax.broadcasted_iota(jnp.int32, (c, c), 1)
    x = ri ^ ci
    masks = []
    s = 1
    while s < c:
        masks.append(((ri > ci) & (x >= s) & (x < 2 * s)).astype(BF16))
        s *= 2
    return masks


def _unit_lower_inverse(a, masks):
    c = a.shape[-1]
    eye = (lax.broadcasted_iota(jnp.int32, (c, c), 0)
           == lax.broadcasted_iota(jnp.int32, (c, c), 1)).astype(F32)
    d = eye - (a * masks[0]).astype(F32)
    for m in masks[1:]:
        y = _bmm(a * m, d)
        d = d - _bmm(d, y)
    return d


def _delta_kernel(q_ref, k_ref, v_ref, z_ref, bg_ref, bgt_ref, nw_ref, h_ref, gpost_ref, wout_ref,
                  o_ref, s_ref, u_s, w_s, qg_s, kt_s, att_s, egl_s):
    c = DELTA_BLOCK
    tt = bg_ref.shape[1]
    nblk = tt // c

    @pl.when(pl.program_id(1) == 0)
    def _():
        s_ref[...] = jnp.zeros(s_ref.shape, F32)

    ri = lax.broadcasted_iota(jnp.int32, (c, c), 0)
    ci = lax.broadcasted_iota(jnp.int32, (c, c), 1)
    causal = ri >= ci
    rep = V_HEADS // K_HEADS
    nch = V_HEADS * nblk

    qj = q_ref[0].reshape(K_HEADS * nblk, c, HEAD_DIM)
    kj = k_ref[0].reshape(K_HEADS * nblk, c, HEAD_DIM)
    prod = jnp.einsum("ncd,nsd->ncs", jnp.concatenate([qj, kj], axis=1), kj,
                      preferred_element_type=F32)
    per_value_head = lambda x: jnp.concatenate(
        [x[g * nblk:(g + 1) * nblk] for g in range(K_HEADS) for _ in range(rep)], axis=0)
    qk = per_value_head(prod[:, :c])
    kk = per_value_head(prod[:, c:])
    qf = per_value_head(qj.astype(F32))
    kf = per_value_head(kj.astype(F32))
    bg = bg_ref[0]
    spread = lambda lane0: jnp.broadcast_to(
        jnp.concatenate([bg[:, lane0 + h:lane0 + h + 1].reshape(nblk, c, 1)
                         for h in range(V_HEADS)], axis=0), (nch, c, LANES))
    beta = spread(0)
    gcol = spread(V_HEADS)
    grow = jnp.stack([bgt_ref[0, V_HEADS + h:V_HEADS + h + 1, n * c:(n + 1) * c]
                      for h in range(V_HEADS) for n in range(nblk)])
    dec = jnp.exp(jnp.where(causal, gcol - grow, -jnp.inf))
    t_inv = _unit_lower_inverse((beta * kk * dec).astype(BF16), _level_masks(c))
    glast = gcol[:, c - 1:c, :]
    eg = jnp.exp(gcol)
    vf = v_ref[0].reshape(nch, c, HEAD_DIM).astype(F32)
    rhs = jnp.concatenate([beta * vf, (beta * eg) * kf], axis=-1)
    uw = _bmm(t_inv, rhs)
    per_head = lambda x: x.reshape(V_HEADS, tt, x.shape[-1])
    u_s[...] = per_head(uw[:, :, :HEAD_DIM])
    w_s[...] = per_head(uw[:, :, HEAD_DIM:]).astype(BF16)
    qg_s[...] = per_head(qf * eg).astype(BF16)
    kt_s[...] = per_head(kf * jnp.exp(glast - gcol)).astype(BF16)
    att_s[...] = per_head(qk * dec).astype(BF16)
    egl_s[...] = jnp.broadcast_to(jnp.exp(glast), (nch, 8, LANES)).reshape(
        V_HEADS, nblk * 8, LANES)

    nw = nw_ref[...]
    state = s_ref[...]
    for n in range(nblk):
        rows = slice(n * c, (n + 1) * c)
        wq = jnp.concatenate([w_s[:, rows, :], qg_s[:, rows, :]], axis=1)
        r = jnp.einsum("hcd,hde->hce", wq, state.astype(BF16), preferred_element_type=F32)
        v_new = (u_s[:, rows, :] - r[:, :c]).astype(BF16)
        o = r[:, c:] + jnp.einsum("hcs,hse->hce", att_s[:, rows, :], v_new,
                                  preferred_element_type=F32)
        decay = egl_s[:, n * 8:n * 8 + 1, :]
        state = state * decay + jnp.einsum("hcd,hce->hde", kt_s[:, rows, :], v_new,
                                           preferred_element_type=F32)
        on = o * lax.rsqrt(jnp.mean(o * o, axis=-1, keepdims=True) + RMS_EPS) * nw
        y = on * _silu(z_ref[0, :, rows, :].astype(F32))
        y = jnp.concatenate([y[h].astype(BF16) for h in range(V_HEADS)], axis=-1)
        mix = _dot(y, wout_ref[...])
        o_ref[0, rows, :] = h_ref[0, rows, :] + _rms(mix, gpost_ref[...])
    s_ref[...] = state


def _delta_rule(h, q, k, v, z, bg, bgt, norm_w, g_post, w_out, tt):
    b, t, d = h.shape
    head_major = lambda nh: pl.BlockSpec((1, nh, tt, HEAD_DIM), lambda i, j: (i, 0, j, 0))
    per_head = lambda width, dtype: pltpu.VMEM((V_HEADS, tt, width), dtype)
    return pl.pallas_call(
        _delta_kernel,
        grid=(b, t // tt),
        in_specs=[
            head_major(K_HEADS),
            head_major(K_HEADS),
            head_major(V_HEADS),
            head_major(V_HEADS),
            pl.BlockSpec((1, tt, LANES), lambda i, j: (i, j, 0)),
            pl.BlockSpec((1, LANES, tt), lambda i, j: (i, 0, j)),
            _resident((1, HEAD_DIM)),
            pl.BlockSpec((1, tt, d), lambda i, j: (i, j, 0)),
            _resident((1, d)),
            _resident(w_out.shape),
        ],
        out_specs=pl.BlockSpec((1, tt, d), lambda i, j: (i, j, 0)),
        out_shape=jax.ShapeDtypeStruct((b, t, d), F32),
        scratch_shapes=[
            pltpu.VMEM((V_HEADS, HEAD_DIM, HEAD_DIM), F32),
            per_head(HEAD_DIM, F32),
            per_head(HEAD_DIM, BF16),
            per_head(HEAD_DIM, BF16),
            per_head(HEAD_DIM, BF16),
            per_head(DELTA_BLOCK, BF16),
            pltpu.VMEM((V_HEADS, 8 * (tt // DELTA_BLOCK), LANES), F32),
        ],
        compiler_params=_params(2),
        name="delta_rule",
    )(q, k, v, z, bg, bgt, norm_w, h, g_post, w_out)


def _pad_lanes(vec, offset):
    return jnp.zeros((1, LANES), F32).at[0, offset:offset + vec.shape[0]].set(vec.astype(F32))


def _gated_deltanet(h, g_pre, g_post, w_in, conv_w, a_log, dt_bias, norm_w, w_out, tt):
    d = h.shape[2]
    w_qkv = w_in[:, :CONV_DIM].astype(BF16)
    w_z = w_in[:, CONV_DIM:CONV_DIM + VAL_DIM].astype(BF16)
    w_ba = jnp.zeros((d, LANES), F32).at[:, :2 * V_HEADS].set(w_in[:, CONV_DIM + VAL_DIM:])
    q, k, v, z, bg, bgt = _gdn_proj(
        h, g_pre, w_qkv, w_z, w_ba.astype(BF16), conv_w.astype(F32),
        _pad_lanes(a_log, V_HEADS), _pad_lanes(dt_bias, V_HEADS), tt)
    return _delta_rule(h, q, k, v, z, bg, bgt, norm_w.astype(F32).reshape(1, HEAD_DIM), g_post,
                       w_out.astype(BF16), _tile(h.shape[1], DELTA_TIME_TILE))


def _tile(n, want):
    return want if n % want == 0 else n


def kernel(x, norm_g, ffn_w_gate_up, ffn_w_down, sc_w_in, sc_conv_w, sc_w_out, gdn_w_in,
           gdn_conv_w, gdn_a_log, gdn_dt_bias, gdn_norm_w, gdn_w_out):
    b, t, d = x.shape
    depth = norm_g.shape[0]
    tt = _tile(t, ROW_TILE)
    tf = _tile(b * t, FFN_ROW_TILE)
    gains = norm_g.astype(F32).reshape(depth, norm_g.shape[1], 1, d)
    h = x.astype(F32)
    for i in range(depth):
        g = gains[i]
        h = _ffn(h.reshape(b * t, d), g[0], g[1], ffn_w_gate_up[i, 0].astype(BF16),
                 ffn_w_down[i, 0].astype(BF16), tf).reshape(b, t, d)
        j = i // 2
        if i % 2 == 0:
            h = _short_conv(h, g[2], g[3], sc_w_in[j].astype(BF16), sc_conv_w[j].astype(F32),
                            sc_w_out[j].astype(BF16), _tile(t, SC_ROW_TILE))
        else:
            h = _gated_deltanet(h, g[2], g[3], gdn_w_in[j], gdn_conv_w[j], gdn_a_log[j],
                                gdn_dt_bias[j], gdn_norm_w[j], gdn_w_out[j], tt)
        h = _ffn(h.reshape(b * t, d), g[4], g[5], ffn_w_gate_up[i, 1].astype(BF16),
                 ffn_w_down[i, 1].astype(BF16), tf).reshape(b, t, d)
    return h
```

```python
import functools

import jax
import jax.numpy as jnp
from jax import lax
from jax.experimental import pallas as pl
from jax.experimental.pallas import tpu as pltpu

F32 = jnp.float32
BF16 = jnp.bfloat16

RMS_EPS = 1e-6
L2_EPS = 1e-6
LANES = 128
HEAD_DIM = 128
K_HEADS = 8
V_HEADS = 16
KEY_DIM = K_HEADS * HEAD_DIM
VAL_DIM = V_HEADS * HEAD_DIM
CONV_DIM = 2 * KEY_DIM + VAL_DIM
GDN_CONV_WIDTH = 4
SC_CONV_WIDTH = 3
CARRY_ROWS = 8
DELTA_BLOCK = 128
DELTA_TIME_TILE = 256
ROW_TILE = 512
FFN_ROW_TILE = 1024
FFN_SUB_ROWS = 256
GDN_PROJ_SUB_ROWS = 512
V7X_VMEM_BYTES = 64 * 1024 * 1024
VMEM_LIMIT_BYTES = V7X_VMEM_BYTES - 8 * 1024 * 1024


def _dot(a, b):
    return jnp.dot(a, b, preferred_element_type=F32)


def _rms(x, g):
    return x * lax.rsqrt(jnp.mean(x * x, axis=-1, keepdims=True) + RMS_EPS) * g


def _sigmoid(x):
    return 1.0 / (1.0 + jnp.exp(-x))


def _silu(x):
    return x * _sigmoid(x)


def _softplus(x):
    return jnp.maximum(x, 0.0) + jnp.log(1.0 + jnp.exp(-jnp.abs(x)))


def _causal_conv_slabs(p, taps, hist_ref, slab0):
    tt, k = p.shape[0], taps.shape[0]
    outs = []
    for j in range(p.shape[1] // LANES):
        lanes = slice(j * LANES, (j + 1) * LANES)
        pj = p[:, lanes]
        hist_ref[slab0 + j, CARRY_ROWS:CARRY_ROWS + tt, :] = pj
        conv = taps[k - 1:k, lanes] * pj
        for back in range(1, k):
            r0 = CARRY_ROWS - back
            conv += taps[k - 1 - back:k - back, lanes] * hist_ref[slab0 + j, r0:r0 + tt, :]
        hist_ref[slab0 + j, 0:CARRY_ROWS, :] = hist_ref[slab0 + j, tt:tt + CARRY_ROWS, :]
        outs.append(conv)
    return jnp.concatenate(outs, axis=-1)


def _resident(shape):
    return pl.BlockSpec(shape, lambda *_: (0,) * len(shape), pipeline_mode=pl.Buffered(1))


def _params(n_grid_dims):
    return pltpu.CompilerParams(
        dimension_semantics=("arbitrary",) * n_grid_dims,
        vmem_limit_bytes=VMEM_LIMIT_BYTES,
    )


def _ffn_kernel(h_ref, gpre_ref, gpost_ref, wgu_ref, wd_ref, o_ref):
    d_ff = wd_ref.shape[0]
    for r0 in range(0, h_ref.shape[0], FFN_SUB_ROWS):
        rows = slice(r0, r0 + FFN_SUB_ROWS)
        x = h_ref[rows, :]
        xn = _rms(x, gpre_ref[...]).astype(BF16)
        gate = _dot(xn, wgu_ref[:, :d_ff])
        up = _dot(xn, wgu_ref[:, d_ff:])
        act = (_silu(gate) * up).astype(BF16)
        ff = _dot(act, wd_ref[...])
        o_ref[rows, :] = x + 0.5 * _rms(ff, gpost_ref[...])


def _ffn(h2d, g_pre, g_post, wgu, wd, tm):
    m, d = h2d.shape
    return pl.pallas_call(
        _ffn_kernel,
        grid=(m // tm,),
        in_specs=[
            pl.BlockSpec((tm, d), lambda i: (i, 0)),
            _resident((1, d)),
            _resident((1, d)),
            _resident(wgu.shape),
            _resident(wd.shape),
        ],
        out_specs=pl.BlockSpec((tm, d), lambda i: (i, 0)),
        out_shape=jax.ShapeDtypeStruct((m, d), F32),
        compiler_params=_params(1),
        name="ffn",
    )(h2d, g_pre, g_post, wgu, wd)


def _sc_kernel(h_ref, gpre_ref, gpost_ref, win_ref, cw_ref, wout_ref, o_ref, carry):
    d = h_ref.shape[2]

    @pl.when(pl.program_id(1) == 0)
    def _():
        carry[:, 0:CARRY_ROWS, :] = jnp.zeros((d // LANES, CARRY_ROWS, LANES), F32)

    x = h_ref[0]
    xn = _rms(x, gpre_ref[...]).astype(BF16)
    proj = _dot(xn, win_ref[...])
    u = proj[:, d:2 * d] * proj[:, 2 * d:]
    conv = _causal_conv_slabs(u, cw_ref[...], carry, 0)
    y = (proj[:, :d] * conv).astype(BF16)
    mix = _dot(y, wout_ref[...])
    o_ref[0] = x + _rms(mix, gpost_ref[...])


def _short_conv(h, g_pre, g_post, w_in, conv_w, w_out, tt):
    b, t, d = h.shape
    return pl.pallas_call(
        _sc_kernel,
        grid=(b, t // tt),
        in_specs=[
            pl.BlockSpec((1, tt, d), lambda i, j: (i, j, 0)),
            _resident((1, d)),
            _resident((1, d)),
            _resident(w_in.shape),
            _resident(conv_w.shape),
            _resident(w_out.shape),
        ],
        out_specs=pl.BlockSpec((1, tt, d), lambda i, j: (i, j, 0)),
        out_shape=jax.ShapeDtypeStruct((b, t, d), F32),
        scratch_shapes=[pltpu.VMEM((d // LANES, tt + CARRY_ROWS, LANES), F32)],
        compiler_params=_params(2),
        name="short_conv",
    )(h, g_pre, g_post, w_in, conv_w, w_out)


def _gdn_proj_kernel(h_ref, gpre_ref, wqkv_ref, wz_ref, wba_ref, cw_ref, alog_ref, dtb_ref,
                     q_ref, k_ref, v_ref, z_ref, bg_ref, bgt_ref, cbuf):
    tt = h_ref.shape[1]

    @pl.when(pl.program_id(1) == 0)
    def _():
        cbuf[:, 0:CARRY_ROWS, :] = jnp.zeros((CONV_DIM // LANES, CARRY_ROWS, LANES), F32)

    sub = min(tt, GDN_PROJ_SUB_ROWS)
    lane = lax.broadcasted_iota(jnp.int32, (sub, LANES), 1)
    row = lax.broadcasted_iota(jnp.int32, (sub, LANES), 0) % DELTA_BLOCK
    is_g = (lane >= V_HEADS) & (lane < 2 * V_HEADS)
    for r0 in range(0, tt, sub):
        rows = slice(r0, r0 + sub)
        xn = _rms(h_ref[0, rows, :], gpre_ref[...]).astype(BF16)
        for cb in range(CONV_DIM // KEY_DIM):
            cols = slice(cb * KEY_DIM, (cb + 1) * KEY_DIM)
            p = _dot(xn, wqkv_ref[:, cols])
            s = _silu(_causal_conv_slabs(p, cw_ref[:, cols], cbuf, cb * (KEY_DIM // LANES)))
            for hd in range(K_HEADS):
                sh = s[:, hd * HEAD_DIM:(hd + 1) * HEAD_DIM]
                if cb < 2:
                    sh = sh * lax.rsqrt(jnp.sum(sh * sh, axis=-1, keepdims=True) + L2_EPS)
                if cb == 0:
                    q_ref[0, hd, rows, :] = (sh * HEAD_DIM ** -0.5).astype(BF16)
                elif cb == 1:
                    k_ref[0, hd, rows, :] = sh.astype(BF16)
                else:
                    v_ref[0, (cb - 2) * K_HEADS + hd, rows, :] = sh.astype(BF16)

        z = _dot(xn, wz_ref[...])
        for hd in range(V_HEADS):
            z_ref[0, hd, rows, :] = z[:, hd * HEAD_DIM:(hd + 1) * HEAD_DIM].astype(BF16)

        ba = _dot(xn, wba_ref[...])
        g = jnp.where(is_g, -jnp.exp(alog_ref[...]) * _softplus(ba + dtb_ref[...]), 0.0)
        shift = 1
        while shift < DELTA_BLOCK:
            g = g + jnp.where(row >= shift, pltpu.roll(g, shift, 0), 0.0)
            shift *= 2
        bg = jnp.where(lane < V_HEADS, _sigmoid(ba), g)
        bg_ref[0, rows, :] = bg
        bgt_ref[0, :, rows] = bg.T


def _gdn_proj(h, g_pre, w_qkv, w_z, w_ba, conv_w, a_log, dt_bias, tt):
    b, t, d = h.shape
    head_major = lambda nh: pl.BlockSpec((1, nh, tt, HEAD_DIM), lambda i, j: (i, 0, j, 0))
    return pl.pallas_call(
        _gdn_proj_kernel,
        grid=(b, t // tt),
        in_specs=[
            pl.BlockSpec((1, tt, d), lambda i, j: (i, j, 0)),
            _resident((1, d)),
            _resident(w_qkv.shape),
            _resident(w_z.shape),
            _resident(w_ba.shape),
            _resident(conv_w.shape),
            _resident((1, LANES)),
            _resident((1, LANES)),
        ],
        out_specs=[
            head_major(K_HEADS),
            head_major(K_HEADS),
            head_major(V_HEADS),
            head_major(V_HEADS),
            pl.BlockSpec((1, tt, LANES), lambda i, j: (i, j, 0)),
            pl.BlockSpec((1, LANES, tt), lambda i, j: (i, 0, j)),
        ],
        out_shape=[
            jax.ShapeDtypeStruct((b, K_HEADS, t, HEAD_DIM), BF16),
            jax.ShapeDtypeStruct((b, K_HEADS, t, HEAD_DIM), BF16),
            jax.ShapeDtypeStruct((b, V_HEADS, t, HEAD_DIM), BF16),
            jax.ShapeDtypeStruct((b, V_HEADS, t, HEAD_DIM), BF16),
            jax.ShapeDtypeStruct((b, t, LANES), F32),
            jax.ShapeDtypeStruct((b, LANES, t), F32),
        ],
        scratch_shapes=[pltpu.VMEM((CONV_DIM // LANES, tt + CARRY_ROWS, LANES), F32)],
        compiler_params=_params(2),
        name="gdn_proj",
    )(h, g_pre, w_qkv, w_z, w_ba, conv_w, a_log, dt_bias)


def _bmm(a, b):
    return jnp.einsum("nij,njk->nik", a.astype(BF16), b.astype(BF16), preferred_element_type=F32)


def _level_masks(c):
    ri = lax.broadcasted_iota(jnp.int32, (c, c), 0)
    ci = lax.broadcasted_iota(jnp.int32, (c, c), 1)
    x = ri ^ ci
    masks = []
    s = 1
    while s < c:
        masks.append(((ri > ci) & (x >= s) & (x < 2 * s)).astype(BF16))
        s *= 2
    return masks


def _unit_lower_inverse(a, masks):
    c = a.shape[-1]
    eye = (lax.broadcasted_iota(jnp.int32, (c, c), 0)
           == lax.broadcasted_iota(jnp.int32, (c, c), 1)).astype(F32)
    d = eye - (a * masks[0]).astype(F32)
    for m in masks[1:]:
        y = _bmm(a * m, d)
        d = d - _bmm(d, y)
    return d


def _delta_kernel(q_ref, k_ref, v_ref, z_ref, bg_ref, bgt_ref, nw_ref, h_ref, gpost_ref, wout_ref,
                  o_ref, s_ref, u_s, w_s, qg_s, kt_s, att_s, egl_s):
    c = DELTA_BLOCK
    tt = bg_ref.shape[1]
    nblk = tt // c

    @pl.when(pl.program_id(1) == 0)
    def _():
        s_ref[...] = jnp.zeros(s_ref.shape, F32)

    ri = lax.broadcasted_iota(jnp.int32, (c, c), 0)
    ci = lax.broadcasted_iota(jnp.int32, (c, c), 1)
    causal = ri >= ci
    rep = V_HEADS // K_HEADS
    nch = V_HEADS * nblk

    qj = q_ref[0].reshape(K_HEADS * nblk, c, HEAD_DIM)
    kj = k_ref[0].reshape(K_HEADS * nblk, c, HEAD_DIM)
    prod = jnp.einsum("ncd,nsd->ncs", jnp.concatenate([qj, kj], axis=1), kj,
                      preferred_element_type=F32)
    per_value_head = lambda x: jnp.concatenate(
        [x[g * nblk:(g + 1) * nblk] for g in range(K_HEADS) for _ in range(rep)], axis=0)
    qk = per_value_head(prod[:, :c])
    kk = per_value_head(prod[:, c:])
    qf = per_value_head(qj.astype(F32))
    kf = per_value_head(kj.astype(F32))
    bg = bg_ref[0]
    spread = lambda lane0: jnp.broadcast_to(
        jnp.concatenate([bg[:, lane0 + h:lane0 + h + 1].reshape(nblk, c, 1)
                         for h in range(V_HEADS)], axis=0), (nch, c, LANES))
    beta = spread(0)
    gcol = spread(V_HEADS)
    grow = jnp.stack([bgt_ref[0, V_HEADS + h:V_HEADS + h + 1, n * c:(n + 1) * c]
                      for h in range(V_HEADS) for n in range(nblk)])
    dec = jnp.exp(jnp.where(causal, gcol - grow, -jnp.inf))
    t_inv = _unit_lower_inverse((beta * kk * dec).astype(BF16), _level_masks(c))
    glast = gcol[:, c - 1:c, :]
    eg = jnp.exp(gcol)
    vf = v_ref[0].reshape(nch, c, HEAD_DIM).astype(F32)
    rhs = jnp.concatenate([beta * vf, (beta * eg) * kf], axis=-1)
    uw = _bmm(t_inv, rhs)
    per_head = lambda x: x.reshape(V_HEADS, tt, x.shape[-1])
    u_s[...] = per_head(uw[:, :, :HEAD_DIM])
    w_s[...] = per_head(uw[:, :, HEAD_DIM:]).astype(BF16)
    qg_s[...] = per_head(qf * eg).astype(BF16)
    kt_s[...] = per_head(kf * jnp.exp(glast - gcol)).astype(BF16)
    att_s[...] = per_head(qk * dec).astype(BF16)
    egl_s[...] = jnp.broadcast_to(jnp.exp(glast), (nch, 8, LANES)).reshape(
        V_HEADS, nblk * 8, LANES)

    nw = nw_ref[...]
    state = s_ref[...]
    for n in range(nblk):
        rows = slice(n * c, (n + 1) * c)
        wq = jnp.concatenate([w_s[:, rows, :], qg_s[:, rows, :]], axis=1)
        r = jnp.einsum("hcd,hde->hce", wq, state.astype(BF16), preferred_element_type=F32)
        v_new = (u_s[:, rows, :] - r[:, :c]).astype(BF16)
        o = r[:, c:] + jnp.einsum("hcs,hse->hce", att_s[:, rows, :], v_new,
                                  preferred_element_type=F32)
        decay = egl_s[:, n * 8:n * 8 + 1, :]
        state = state * decay + jnp.einsum("hcd,hce->hde", kt_s[:, rows, :], v_new,
                                           preferred_element_type=F32)
        on = o * lax.rsqrt(jnp.mean(o * o, axis=-1, keepdims=True) + RMS_EPS) * nw
        y = on * _silu(z_ref[0, :, rows, :].astype(F32))
        y = jnp.concatenate([y[h].astype(BF16) for h in range(V_HEADS)], axis=-1)
        mix = _dot(y, wout_ref[...])
        o_ref[0, rows, :] = h_ref[0, rows, :] + _rms(mix, gpost_ref[...])
    s_ref[...] = state


def _delta_rule(h, q, k, v, z, bg, bgt, norm_w, g_post, w_out, tt):
    b, t, d = h.shape
    head_major = lambda nh: pl.BlockSpec((1, nh, tt, HEAD_DIM), lambda i, j: (i, 0, j, 0))
    per_head = lambda width, dtype: pltpu.VMEM((V_HEADS, tt, width), dtype)
    return pl.pallas_call(
        _delta_kernel,
        grid=(b, t // tt),
        in_specs=[
            head_major(K_HEADS),
            head_major(K_HEADS),
            head_major(V_HEADS),
            head_major(V_HEADS),
            pl.BlockSpec((1, tt, LANES), lambda i, j: (i, j, 0)),
            pl.BlockSpec((1, LANES, tt), lambda i, j: (i, 0, j)),
            _resident((1, HEAD_DIM)),
            pl.BlockSpec((1, tt, d), lambda i, j: (i, j, 0)),
            _resident((1, d)),
            _resident(w_out.shape),
        ],
        out_specs=pl.BlockSpec((1, tt, d), lambda i, j: (i, j, 0)),
        out_shape=jax.ShapeDtypeStruct((b, t, d), F32),
        scratch_shapes=[
            pltpu.VMEM((V_HEADS, HEAD_DIM, HEAD_DIM), F32),
            per_head(HEAD_DIM, F32),
            per_head(HEAD_DIM, BF16),
            per_head(HEAD_DIM, BF16),
            per_head(HEAD_DIM, BF16),
            per_head(DELTA_BLOCK, BF16),
            pltpu.VMEM((V_HEADS, 8 * (tt // DELTA_BLOCK), LANES), F32),
        ],
        compiler_params=_params(2),
        name="delta_rule",
    )(q, k, v, z, bg, bgt, norm_w, h, g_post, w_out)


def _pad_lanes(vec, offset):
    return jnp.zeros((1, LANES), F32).at[0, offset:offset + vec.shape[0]].set(vec.astype(F32))


def _gated_deltanet(h, g_pre, g_post, w_in, conv_w, a_log, dt_bias, norm_w, w_out, tt):
    d = h.shape[2]
    w_qkv = w_in[:, :CONV_DIM].astype(BF16)
    w_z = w_in[:, CONV_DIM:CONV_DIM + VAL_DIM].astype(BF16)
    w_ba = jnp.zeros((d, LANES), F32).at[:, :2 * V_HEADS].set(w_in[:, CONV_DIM + VAL_DIM:])
    q, k, v, z, bg, bgt = _gdn_proj(
        h, g_pre, w_qkv, w_z, w_ba.astype(BF16), conv_w.astype(F32),
        _pad_lanes(a_log, V_HEADS), _pad_lanes(dt_bias, V_HEADS), tt)
    return _delta_rule(h, q, k, v, z, bg, bgt, norm_w.astype(F32).reshape(1, HEAD_DIM), g_post,
                       w_out.astype(BF16), _tile(h.shape[1], DELTA_TIME_TILE))


def _tile(n, want):
    return want if n % want == 0 else n


def kernel(x, norm_g, ffn_w_gate_up, ffn_w_down, sc_w_in, sc_conv_w, sc_w_out, gdn_w_in,
           gdn_conv_w, gdn_a_log, gdn_dt_bias, gdn_norm_w, gdn_w_out):
    b, t, d = x.shape
    depth = norm_g.shape[0]
    tt = _tile(t, ROW_TILE)
    tf = _tile(b * t, FFN_ROW_TILE)
    gains = norm_g.astype(F32).reshape(depth, norm_g.shape[1], 1, d)
    h = x.astype(F32)
    for i in range(depth):
        g = gains[i]
        h = _ffn(h.reshape(b * t, d), g[0], g[1], ffn_w_gate_up[i, 0].astype(BF16),
                 ffn_w_down[i, 0].astype(BF16), tf).reshape(b, t, d)
        j = i // 2
        if i % 2 == 0:
            h = _short_conv(h, g[2], g[3], sc_w_in[j].astype(BF16), sc_conv_w[j].astype(F32),
                            sc_w_out[j].astype(BF16), tt)
        else:
            h = _gated_deltanet(h, g[2], g[3], gdn_w_in[j], gdn_conv_w[j], gdn_a_log[j],
                                gdn_dt_bias[j], gdn_norm_w[j], gdn_w_out[j], tt)
        h = _ffn(h.reshape(b * t, d), g[4], g[5], ffn_w_gate_up[i, 1].astype(BF16),
                 ffn_w_down[i, 1].astype(BF16), tf).reshape(b, t, d)
    return h
```

```python
import functools

import jax
import jax.numpy as jnp
from jax import lax
from jax.experimental import pallas as pl
from jax.experimental.pallas import tpu as pltpu

F32 = jnp.float32
BF16 = jnp.bfloat16

RMS_EPS = 1e-6
L2_EPS = 1e-6
LANES = 128
HEAD_DIM = 128
K_HEADS = 8
V_HEADS = 16
KEY_DIM = K_HEADS * HEAD_DIM
VAL_DIM = V_HEADS * HEAD_DIM
CONV_DIM = 2 * KEY_DIM + VAL_DIM
GDN_CONV_WIDTH = 4
SC_CONV_WIDTH = 3
CARRY_ROWS = 8
DELTA_BLOCK = 128
DELTA_TIME_TILE = 256
ROW_TILE = 512
FFN_ROW_TILE = 1024
FFN_SUB_ROWS = 256
GDN_PROJ_SUB_ROWS = 256
V7X_VMEM_BYTES = 64 * 1024 * 1024
VMEM_LIMIT_BYTES = V7X_VMEM_BYTES - 8 * 1024 * 1024


def _dot(a, b):
    return jnp.dot(a, b, preferred_element_type=F32)


def _rms(x, g):
    return x * lax.rsqrt(jnp.mean(x * x, axis=-1, keepdims=True) + RMS_EPS) * g


def _sigmoid(x):
    return 1.0 / (1.0 + jnp.exp(-x))


def _silu(x):
    return x * _sigmoid(x)


def _softplus(x):
    return jnp.maximum(x, 0.0) + jnp.log(1.0 + jnp.exp(-jnp.abs(x)))


def _causal_conv_slabs(p, taps, hist_ref, slab0):
    tt, k = p.shape[0], taps.shape[0]
    outs = []
    for j in range(p.shape[1] // LANES):
        lanes = slice(j * LANES, (j + 1) * LANES)
        pj = p[:, lanes]
        hist_ref[slab0 + j, CARRY_ROWS:CARRY_ROWS + tt, :] = pj
        conv = taps[k - 1:k, lanes] * pj
        for back in range(1, k):
            r0 = CARRY_ROWS - back
            conv += taps[k - 1 - back:k - back, lanes] * hist_ref[slab0 + j, r0:r0 + tt, :]
        hist_ref[slab0 + j, 0:CARRY_ROWS, :] = hist_ref[slab0 + j, tt:tt + CARRY_ROWS, :]
        outs.append(conv)
    return jnp.concatenate(outs, axis=-1)


def _resident(shape):
    return pl.BlockSpec(shape, lambda *_: (0,) * len(shape), pipeline_mode=pl.Buffered(1))


def _params(n_grid_dims):
    return pltpu.CompilerParams(
        dimension_semantics=("arbitrary",) * n_grid_dims,
        vmem_limit_bytes=VMEM_LIMIT_BYTES,
    )


def _ffn_kernel(h_ref, gpre_ref, gpost_ref, wgu_ref, wd_ref, o_ref):
    d_ff = wd_ref.shape[0]
    for r0 in range(0, h_ref.shape[0], FFN_SUB_ROWS):
        rows = slice(r0, r0 + FFN_SUB_ROWS)
        x = h_ref[rows, :]
        xn = _rms(x, gpre_ref[...]).astype(BF16)
        gate = _dot(xn, wgu_ref[:, :d_ff])
        up = _dot(xn, wgu_ref[:, d_ff:])
        act = (_silu(gate) * up).astype(BF16)
        ff = _dot(act, wd_ref[...])
        o_ref[rows, :] = x + 0.5 * _rms(ff, gpost_ref[...])


def _ffn(h2d, g_pre, g_post, wgu, wd, tm):
    m, d = h2d.shape
    return pl.pallas_call(
        _ffn_kernel,
        grid=(m // tm,),
        in_specs=[
            pl.BlockSpec((tm, d), lambda i: (i, 0)),
            _resident((1, d)),
            _resident((1, d)),
            _resident(wgu.shape),
            _resident(wd.shape),
        ],
        out_specs=pl.BlockSpec((tm, d), lambda i: (i, 0)),
        out_shape=jax.ShapeDtypeStruct((m, d), F32),
        compiler_params=_params(1),
        name="ffn",
    )(h2d, g_pre, g_post, wgu, wd)


def _sc_kernel(h_ref, gpre_ref, gpost_ref, win_ref, cw_ref, wout_ref, o_ref, carry):
    d = h_ref.shape[2]

    @pl.when(pl.program_id(1) == 0)
    def _():
        carry[:, 0:CARRY_ROWS, :] = jnp.zeros((d // LANES, CARRY_ROWS, LANES), F32)

    x = h_ref[0]
    xn = _rms(x, gpre_ref[...]).astype(BF16)
    proj = _dot(xn, win_ref[...])
    u = proj[:, d:2 * d] * proj[:, 2 * d:]
    conv = _causal_conv_slabs(u, cw_ref[...], carry, 0)
    y = (proj[:, :d] * conv).astype(BF16)
    mix = _dot(y, wout_ref[...])
    o_ref[0] = x + _rms(mix, gpost_ref[...])


def _short_conv(h, g_pre, g_post, w_in, conv_w, w_out, tt):
    b, t, d = h.shape
    return pl.pallas_call(
        _sc_kernel,
        grid=(b, t // tt),
        in_specs=[
            pl.BlockSpec((1, tt, d), lambda i, j: (i, j, 0)),
            _resident((1, d)),
            _resident((1, d)),
            _resident(w_in.shape),
            _resident(conv_w.shape),
            _resident(w_out.shape),
        ],
        out_specs=pl.BlockSpec((1, tt, d), lambda i, j: (i, j, 0)),
        out_shape=jax.ShapeDtypeStruct((b, t, d), F32),
        scratch_shapes=[pltpu.VMEM((d // LANES, tt + CARRY_ROWS, LANES), F32)],
        compiler_params=_params(2),
        name="short_conv",
    )(h, g_pre, g_post, w_in, conv_w, w_out)


def _gdn_proj_kernel(h_ref, gpre_ref, wqkv_ref, wz_ref, wba_ref, cw_ref, alog_ref, dtb_ref,
                     q_ref, k_ref, v_ref, z_ref, bg_ref, bgt_ref, cbuf):
    tt = h_ref.shape[1]

    @pl.when(pl.program_id(1) == 0)
    def _():
        cbuf[:, 0:CARRY_ROWS, :] = jnp.zeros((CONV_DIM // LANES, CARRY_ROWS, LANES), F32)

    sub = min(tt, GDN_PROJ_SUB_ROWS)
    lane = lax.broadcasted_iota(jnp.int32, (sub, LANES), 1)
    row = lax.broadcasted_iota(jnp.int32, (sub, LANES), 0) % DELTA_BLOCK
    is_g = (lane >= V_HEADS) & (lane < 2 * V_HEADS)
    for r0 in range(0, tt, sub):
        rows = slice(r0, r0 + sub)
        xn = _rms(h_ref[0, rows, :], gpre_ref[...]).astype(BF16)
        for cb in range(CONV_DIM // KEY_DIM):
            cols = slice(cb * KEY_DIM, (cb + 1) * KEY_DIM)
            p = _dot(xn, wqkv_ref[:, cols])
            s = _silu(_causal_conv_slabs(p, cw_ref[:, cols], cbuf, cb * (KEY_DIM // LANES)))
            for hd in range(K_HEADS):
                sh = s[:, hd * HEAD_DIM:(hd + 1) * HEAD_DIM]
                if cb < 2:
                    sh = sh * lax.rsqrt(jnp.sum(sh * sh, axis=-1, keepdims=True) + L2_EPS)
                if cb == 0:
                    q_ref[0, hd, rows, :] = (sh * HEAD_DIM ** -0.5).astype(BF16)
                elif cb == 1:
                    k_ref[0, hd, rows, :] = sh.astype(BF16)
                else:
                    v_ref[0, (cb - 2) * K_HEADS + hd, rows, :] = sh.astype(BF16)

        z = _dot(xn, wz_ref[...])
        for hd in range(V_HEADS):
            z_ref[0, hd, rows, :] = z[:, hd * HEAD_DIM:(hd + 1) * HEAD_DIM].astype(BF16)

        ba = _dot(xn, wba_ref[...])
        g = jnp.where(is_g, -jnp.exp(alog_ref[...]) * _softplus(ba + dtb_ref[...]), 0.0)
        shift = 1
        while shift < DELTA_BLOCK:
            g = g + jnp.where(row >= shift, pltpu.roll(g, shift, 0), 0.0)
            shift *= 2
        bg = jnp.where(lane < V_HEADS, _sigmoid(ba), g)
        bg_ref[0, rows, :] = bg
        bgt_ref[0, :, rows] = bg.T


def _gdn_proj(h, g_pre, w_qkv, w_z, w_ba, conv_w, a_log, dt_bias, tt):
    b, t, d = h.shape
    head_major = lambda nh: pl.BlockSpec((1, nh, tt, HEAD_DIM), lambda i, j: (i, 0, j, 0))
    return pl.pallas_call(
        _gdn_proj_kernel,
        grid=(b, t // tt),
        in_specs=[
            pl.BlockSpec((1, tt, d), lambda i, j: (i, j, 0)),
            _resident((1, d)),
            _resident(w_qkv.shape),
            _resident(w_z.shape),
            _resident(w_ba.shape),
            _resident(conv_w.shape),
            _resident((1, LANES)),
            _resident((1, LANES)),
        ],
        out_specs=[
            head_major(K_HEADS),
            head_major(K_HEADS),
            head_major(V_HEADS),
            head_major(V_HEADS),
            pl.BlockSpec((1, tt, LANES), lambda i, j: (i, j, 0)),
            pl.BlockSpec((1, LANES, tt), lambda i, j: (i, 0, j)),
        ],
        out_shape=[
            jax.ShapeDtypeStruct((b, K_HEADS, t, HEAD_DIM), BF16),
            jax.ShapeDtypeStruct((b, K_HEADS, t, HEAD_DIM), BF16),
            jax.ShapeDtypeStruct((b, V_HEADS, t, HEAD_DIM), BF16),
            jax.ShapeDtypeStruct((b, V_HEADS, t, HEAD_DIM), BF16),
            jax.ShapeDtypeStruct((b, t, LANES), F32),
            jax.ShapeDtypeStruct((b, LANES, t), F32),
        ],
        scratch_shapes=[pltpu.VMEM((CONV_DIM // LANES, tt + CARRY_ROWS, LANES), F32)],
        compiler_params=_params(2),
        name="gdn_proj",
    )(h, g_pre, w_qkv, w_z, w_ba, conv_w, a_log, dt_bias)


def _bmm(a, b):
    return jnp.einsum("nij,njk->nik", a.astype(BF16), b.astype(BF16), preferred_element_type=F32)


def _level_masks(c):
    ri = lax.broadcasted_iota(jnp.int32, (c, c), 0)
    ci = lax.broadcasted_iota(jnp.int32, (c, c), 1)
    x = ri ^ ci
    masks = []
    s = 1
    while s < c:
        masks.append(((ri > ci) & (x >= s) & (x < 2 * s)).astype(BF16))
        s *= 2
    return masks


def _unit_lower_inverse(a, masks):
    c = a.shape[-1]
    eye = (lax.broadcasted_iota(jnp.int32, (c, c), 0)
           == lax.broadcasted_iota(jnp.int32, (c, c), 1)).astype(F32)
    d = eye - (a * masks[0]).astype(F32)
    for m in masks[1:]:
        y = _bmm(a * m, d)
        d = d - _bmm(d, y)
    return d


def _delta_kernel(q_ref, k_ref, v_ref, z_ref, bg_ref, bgt_ref, nw_ref, h_ref, gpost_ref, wout_ref,
                  o_ref, s_ref, u_s, w_s, qg_s, kt_s, att_s, egl_s):
    c = DELTA_BLOCK
    tt = bg_ref.shape[1]
    nblk = tt // c

    @pl.when(pl.program_id(1) == 0)
    def _():
        s_ref[...] = jnp.zeros(s_ref.shape, F32)

    ri = lax.broadcasted_iota(jnp.int32, (c, c), 0)
    ci = lax.broadcasted_iota(jnp.int32, (c, c), 1)
    causal = ri >= ci
    rep = V_HEADS // K_HEADS
    nch = V_HEADS * nblk

    qj = q_ref[0].reshape(K_HEADS * nblk, c, HEAD_DIM)
    kj = k_ref[0].reshape(K_HEADS * nblk, c, HEAD_DIM)
    prod = jnp.einsum("ncd,nsd->ncs", jnp.concatenate([qj, kj], axis=1), kj,
                      preferred_element_type=F32)
    per_value_head = lambda x: jnp.concatenate(
        [x[g * nblk:(g + 1) * nblk] for g in range(K_HEADS) for _ in range(rep)], axis=0)
    qk = per_value_head(prod[:, :c])
    kk = per_value_head(prod[:, c:])
    qf = per_value_head(qj.astype(F32))
    kf = per_value_head(kj.astype(F32))
    bg = bg_ref[0]
    spread = lambda lane0: jnp.broadcast_to(
        jnp.concatenate([bg[:, lane0 + h:lane0 + h + 1].reshape(nblk, c, 1)
                         for h in range(V_HEADS)], axis=0), (nch, c, LANES))
    beta = spread(0)
    gcol = spread(V_HEADS)
    grow = jnp.stack([bgt_ref[0, V_HEADS + h:V_HEADS + h + 1, n * c:(n + 1) * c]
                      for h in range(V_HEADS) for n in range(nblk)])
    dec = jnp.exp(jnp.where(causal, gcol - grow, -jnp.inf))
    t_inv = _unit_lower_inverse((beta * kk * dec).astype(BF16), _level_masks(c))
    glast = gcol[:, c - 1:c, :]
    eg = jnp.exp(gcol)
    vf = v_ref[0].reshape(nch, c, HEAD_DIM).astype(F32)
    rhs = jnp.concatenate([beta * vf, (beta * eg) * kf], axis=-1)
    uw = _bmm(t_inv, rhs)
    per_head = lambda x: x.reshape(V_HEADS, tt, x.shape[-1])
    u_s[...] = per_head(uw[:, :, :HEAD_DIM])
    w_s[...] = per_head(uw[:, :, HEAD_DIM:]).astype(BF16)
    qg_s[...] = per_head(qf * eg).astype(BF16)
    kt_s[...] = per_head(kf * jnp.exp(glast - gcol)).astype(BF16)
    att_s[...] = per_head(qk * dec).astype(BF16)
    egl_s[...] = jnp.broadcast_to(jnp.exp(glast), (nch, 8, LANES)).reshape(
        V_HEADS, nblk * 8, LANES)

    nw = nw_ref[...]
    state = s_ref[...]
    for n in range(nblk):
        rows = slice(n * c, (n + 1) * c)
        wq = jnp.concatenate([w_s[:, rows, :], qg_s[:, rows, :]], axis=1)
        r = jnp.einsum("hcd,hde->hce", wq, state.astype(BF16), preferred_element_type=F32)
        v_new = (u_s[:, rows, :] - r[:, :c]).astype(BF16)
        o = r[:, c:] + jnp.einsum("hcs,hse->hce", att_s[:, rows, :], v_new,
                                  preferred_element_type=F32)
        decay = egl_s[:, n * 8:n * 8 + 1, :]
        state = state * decay + jnp.einsum("hcd,hce->hde", kt_s[:, rows, :], v_new,
                                           preferred_element_type=F32)
        on = o * lax.rsqrt(jnp.mean(o * o, axis=-1, keepdims=True) + RMS_EPS) * nw
        y = on * _silu(z_ref[0, :, rows, :].astype(F32))
        y = jnp.concatenate([y[h].astype(BF16) for h in range(V_HEADS)], axis=-1)
        mix = _dot(y, wout_ref[...])
        o_ref[0, rows, :] = h_ref[0, rows, :] + _rms(mix, gpost_ref[...])
    s_ref[...] = state


def _delta_rule(h, q, k, v, z, bg, bgt, norm_w, g_post, w_out, tt):
    b, t, d = h.shape
    head_major = lambda nh: pl.BlockSpec((1, nh, tt, HEAD_DIM), lambda i, j: (i, 0, j, 0))
    per_head = lambda width, dtype: pltpu.VMEM((V_HEADS, tt, width), dtype)
    return pl.pallas_call(
        _delta_kernel,
        grid=(b, t // tt),
        in_specs=[
            head_major(K_HEADS),
            head_major(K_HEADS),
            head_major(V_HEADS),
            head_major(V_HEADS),
            pl.BlockSpec((1, tt, LANES), lambda i, j: (i, j, 0)),
            pl.BlockSpec((1, LANES, tt), lambda i, j: (i, 0, j)),
            _resident((1, HEAD_DIM)),
            pl.BlockSpec((1, tt, d), lambda i, j: (i, j, 0)),
            _resident((1, d)),
            _resident(w_out.shape),
        ],
        out_specs=pl.BlockSpec((1, tt, d), lambda i, j: (i, j, 0)),
        out_shape=jax.ShapeDtypeStruct((b, t, d), F32),
        scratch_shapes=[
            pltpu.VMEM((V_HEADS, HEAD_DIM, HEAD_DIM), F32),
            per_head(HEAD_DIM, F32),
            per_head(HEAD_DIM, BF16),
            per_head(HEAD_DIM, BF16),
            per_head(HEAD_DIM, BF16),
            per_head(DELTA_BLOCK, BF16),
            pltpu.VMEM((V_HEADS, 8 * (tt // DELTA_BLOCK), LANES), F32),
        ],
        compiler_params=_params(2),
        name="delta_rule",
    )(q, k, v, z, bg, bgt, norm_w, h, g_post, w_out)


def _pad_lanes(vec, offset):
    return jnp.zeros((1, LANES), F32).at[0, offset:offset + vec.shape[0]].set(vec.astype(F32))


def _gated_deltanet(h, g_pre, g_post, w_in, conv_w, a_log, dt_bias, norm_w, w_out, tt):
    d = h.shape[2]
    w_qkv = w_in[:, :CONV_DIM].astype(BF16)
    w_z = w_in[:, CONV_DIM:CONV_DIM + VAL_DIM].astype(BF16)
    w_ba = jnp.zeros((d, LANES), F32).at[:, :2 * V_HEADS].set(w_in[:, CONV_DIM + VAL_DIM:])
    q, k, v, z, bg, bgt = _gdn_proj(
        h, g_pre, w_qkv, w_z, w_ba.astype(BF16), conv_w.astype(F32),
        _pad_lanes(a_log, V_HEADS), _pad_lanes(dt_bias, V_HEADS), tt)
    return _delta_rule(h, q, k, v, z, bg, bgt, norm_w.astype(F32).reshape(1, HEAD_DIM), g_post,
                       w_out.astype(BF16), _tile(h.shape[1], DELTA_TIME_TILE))


def _tile(n, want):
    return want if n % want == 0 else n


def kernel(x, norm_g, ffn_w_gate_up, ffn_w_down, sc_w_in, sc_conv_w, sc_w_out, gdn_w_in,
           gdn_conv_w, gdn_a_log, gdn_dt_bias, gdn_norm_w, gdn_w_out):
    b, t, d = x.shape
    depth = norm_g.shape[0]
    tt = _tile(t, ROW_TILE)
    tf = _tile(b * t, FFN_ROW_TILE)
    gains = norm_g.astype(F32).reshape(depth, norm_g.shape[1], 1, d)
    h = x.astype(F32)
    for i in range(depth):
        g = gains[i]
        h = _ffn(h.reshape(b * t, d), g[0], g[1], ffn_w_gate_up[i, 0].astype(BF16),
                 ffn_w_down[i, 0].astype(BF16), tf).reshape(b, t, d)
        j = i // 2
        if i % 2 == 0:
            h = _short_conv(h, g[2], g[3], sc_w_in[j].astype(BF16), sc_conv_w[j].astype(F32),
                            sc_w_out[j].astype(BF16), tt)
        else:
            h = _gated_deltanet(h, g[2], g[3], gdn_w_in[j], gdn_conv_w[j], gdn_a_log[j],
                                gdn_dt_bias[j], gdn_norm_w[j], gdn_w_out[j], tt)
        h = _ffn(h.reshape(b * t, d), g[4], g[5], ffn_w_gate_up[i, 1].astype(BF16),
                 ffn_w_down[i, 1].astype(BF16), tf).reshape(b, t, d)
    return h
```
